```python
import math
import jax, jax.numpy as jnp
from jax import lax
import numpy as np

D_MODEL = 2048
BATCH = 32
SEQ = 256
DEPTH = 4
DEC_BATCH = 4
DEC_SEQ = 1024
PAST_LEN = 512

GRID_W = 64
EPS = 1e-6

HY_W = 512
HY_ORDER = 2
HY_PROJ = (HY_ORDER + 1) * HY_W
POS_BANDS = 16
POS_EMB = 2 * POS_BANDS + 1
FILT_HID = 64
FILTER_OUT_SCALE = 0.1
FAST_DECAY_PCT = 0.3
SLOW_DECAY_PCT = 1.5
DECAY_TARGET = 1e-2

MLA_HEADS = 8
Q_RANK = 512
KV_RANK = 256
QK_NOPE = 128
QK_ROPE = 64
V_DIM = 128
MLA_W = MLA_HEADS * V_DIM
ROPE_THETA = 10000.0
Q_BLOCK = 128

S5_W = 512
S5_GROUP_CH = 16
S5_GROUPS = S5_W // S5_GROUP_CH
S5_STATE = 64
DT_MIN = 1e-3
DT_MAX = 1e-1

MIX_W = HY_W + MLA_W + S5_W
IN_W = HY_PROJ + Q_RANK + KV_RANK + QK_ROPE + S5_W
IN_SPLITS = (HY_PROJ, HY_PROJ + Q_RANK, HY_PROJ + Q_RANK + KV_RANK,
             HY_PROJ + Q_RANK + KV_RANK + QK_ROPE)

N_EXPERTS = 32
TOP_K = 4
D_EXPERT = D_MODEL
SWIGLU_LIMIT = 7.0
SWIGLU_ALPHA = 1.702
EXPERT_BLOCK = 128

kernel_name = "hybrid_hyena_mla_s5_moe_diffusion_step"


def rmsnorm(x, g):
    xf = x.astype(jnp.float32)
    y = xf * lax.rsqrt(jnp.mean(xf * xf, axis=-1, keepdims=True) + EPS)
    return (y * g.astype(jnp.float32)).astype(x.dtype)


def axial_rope_tables(n_tokens):
    rows = n_tokens // GRID_W
    row = jnp.repeat(jnp.arange(rows, dtype=jnp.float32), GRID_W)
    col = jnp.tile(jnp.arange(GRID_W, dtype=jnp.float32), rows)
    half = QK_ROPE // 2
    inv = 1.0 / (ROPE_THETA ** (jnp.arange(0, half, 2, dtype=jnp.float32) / half))
    ar = row[:, None] * inv
    ac = col[:, None] * inv
    ang = jnp.concatenate([ar, ar, ac, ac], axis=-1)
    return jnp.cos(ang), jnp.sin(ang)


def apply_axial_rope(x, cos, sin):
    r1, r2, c1, c2 = jnp.split(x, 4, axis=-1)
    rot = jnp.concatenate([-r2, r1, -c2, c1], axis=-1)
    return (x * cos + rot * sin).astype(x.dtype)


def centred_conv3(z, w, b):
    zp = jnp.pad(z, ((0, 0), (1, 1), (0, 0)))
    return zp[:, :-2] * w[0] + zp[:, 1:-1] * w[1] + zp[:, 2:] * w[2] + b


def hyena_filter_spectra(n, fw1, fb1, fw2, fb2, freq, fw3, decay):
    f32 = jnp.float32
    t = jnp.linspace(0.0, 1.0, n, dtype=f32)[:, None]
    w = (2.0 * math.pi / n) * jnp.arange(n, dtype=f32)[:, None]
    bands = jnp.linspace(1e-4, POS_BANDS - 1, POS_BANDS, dtype=f32)
    feat = jnp.concatenate([t, jnp.cos(w * bands), -jnp.sin(w * bands)], axis=-1)
    h = jnp.sin(freq[0] * (feat @ fw1 + fb1))
    h = jnp.sin(freq[1] * (h @ fw2 + fb2))
    h = (h @ fw3) * jnp.exp(-t * jnp.abs(decay))
    h = h.astype(f32).reshape(n, 2, HY_ORDER, HY_W)
    fwd, bwd = h[:, 0], h[:, 1]
    k = jnp.concatenate([fwd, jnp.zeros_like(fwd[:1]), bwd[:0:-1]], axis=0)
    return jnp.fft.rfft(k, axis=0)


def long_conv(u, k_spec, skip):
    n = u.shape[1]
    u32 = u.astype(jnp.float32)
    y = jnp.fft.irfft(jnp.fft.rfft(u32, n=2 * n, axis=1) * k_spec, n=2 * n, axis=1)[:, :n]
    return y + u32 * skip


def hyena_mixer(z, p):
    z = centred_conv3(z, p["hy_conv_w"], p["hy_conv_b"])
    v, x1, x2 = jnp.split(z, 3, axis=-1)
    spec = hyena_filter_spectra(z.shape[1], p["hy_fw1"], p["hy_fb1"], p["hy_fw2"], p["hy_fb2"],
                                p["hy_freq"], p["hy_fw3"], p["hy_decay"])
    y = x1 * long_conv(v, spec[:, 0], p["hy_skip"][0])
    y = x2 * long_conv(y, spec[:, 1], p["hy_skip"][1])
    return y.astype(z.dtype)


def mla_attend(q_nope, q_rope, k_nope, k_rope, v):
    b, lq, h, _ = q_nope.shape
    nb = lq // Q_BLOCK
    scale = (QK_NOPE + QK_ROPE) ** -0.5

    def to_blocks(t):
        return jnp.moveaxis(t.reshape(b, nb, Q_BLOCK, *t.shape[2:]), 1, 0)

    def block(qs):
        qn, qr = qs
        s = (jnp.einsum("bqhd,bkhd->bhqk", qn, k_nope)
             + jnp.einsum("bqhr,bkr->bhqk", qr, k_rope))
        pr = jax.nn.softmax(s.astype(jnp.float32) * scale, axis=-1).astype(v.dtype)
        return jnp.einsum("bhqk,bkhd->bqhd", pr, v)

    o = lax.map(block, (to_blocks(q_nope), to_blocks(q_rope)))
    return jnp.moveaxis(o, 0, 1).reshape(b, lq, h * V_DIM)


def s5_discretise(lam_re, lam_im, log_dt, b_re, b_im):
    f32 = jnp.float32
    lam = lax.complex(jnp.minimum(lam_re.astype(f32), -1e-4), lam_im.astype(f32))
    dt = jnp.exp(log_dt.astype(f32))[:, None]
    lam_bar = jnp.exp(lam * dt)
    b_bar = ((lam_bar - 1.0) / lam)[..., None] * lax.complex(b_re.astype(f32), b_im.astype(f32))
    return lam_bar, b_bar


def _linear_recurrence(e1, e2):
    a1, b1 = e1
    a2, b2 = e2
    return a1 * a2, a2 * b1 + b2


def s5_scan(u, lam_bar, b_bar, x0, reverse):
    bu = jnp.einsum("gpn,blgn->blgp", b_bar, u)
    edge = bu.shape[1] - 1 if reverse else 0
    bu = bu.at[:, edge].add(lam_bar * x0)
    a = jnp.broadcast_to(lam_bar, bu.shape)
    _, xs = lax.associative_scan(_linear_recurrence, (a, bu), reverse=reverse, axis=1)
    return xs


def s5_mixer(u, x0_f, x0_b, p):
    f32 = jnp.float32
    bsz, n, _ = u.shape
    u32 = u.astype(f32)
    uc = u32.reshape(bsz, n, S5_GROUPS, S5_GROUP_CH).astype(jnp.complex64)
    lf, bf = s5_discretise(p["s5_lam_re"][0], p["s5_lam_im"][0], p["s5_log_dt"][0],
                           p["s5_b_re"][0], p["s5_b_im"][0])
    lb, bb = s5_discretise(p["s5_lam_re"][1], p["s5_lam_im"][1], p["s5_log_dt"][1],
                           p["s5_b_re"][1], p["s5_b_im"][1])
    xf = s5_scan(uc, lf, bf, x0_f, reverse=False)
    xb = s5_scan(uc, lb, bb, x0_b, reverse=True)
    cf = lax.complex(p["s5_c_re"][0].astype(f32), p["s5_c_im"][0].astype(f32))
    cb = lax.complex(p["s5_c_re"][1].astype(f32), p["s5_c_im"][1].astype(f32))
    y = jnp.real(jnp.einsum("gnp,blgp->blgn", cf, xf) + jnp.einsum("gnp,blgp->blgn", cb, xb))
    y = y.reshape(bsz, n, S5_W) + u32 * p["s5_d"]
    zg = jax.nn.gelu(y)
    out = zg * jax.nn.sigmoid(zg @ p["s5_w_glu"] + p["s5_b_glu"])
    return out.astype(u.dtype), xf[:, -1], xb[:, 0]


def routed_ffn(h, w_r, b_r, w_gu, b_gu, w_dn, b_dn):
    hf = h.reshape(-1, h.shape[-1])
    t = hf.shape[0]
    logits = (hf @ w_r + b_r).astype(jnp.float32)
    top_v, top_i = lax.top_k(logits, TOP_K)
    gates = jax.nn.softmax(top_v, axis=-1)
    n_assign = t * TOP_K
    flat_e = top_i.reshape(n_assign)
    order = jnp.argsort(flat_e)
    sorted_e = flat_e[order]
    tok = order // TOP_K
    counts = jnp.zeros((N_EXPERTS,), jnp.int32).at[flat_e].add(1)
    padded = (counts + EXPERT_BLOCK - 1) // EXPERT_BLOCK * EXPERT_BLOCK
    pad_end = jnp.cumsum(padded)
    pad_start = pad_end - padded
    start = jnp.cumsum(counts) - counts
    dest = pad_start[sorted_e] + jnp.arange(n_assign, dtype=jnp.int32) - start[sorted_e]
    n_blk = -(-n_assign // EXPERT_BLOCK) + N_EXPERTS
    slot_tok = jnp.zeros((n_blk * EXPERT_BLOCK,), jnp.int32).at[dest].set(tok)
    blk_exp = jnp.minimum(
        jnp.searchsorted(pad_end, jnp.arange(n_blk, dtype=jnp.int32) * EXPERT_BLOCK, side="right"),
        N_EXPERTS - 1)
    xs = hf[slot_tok].reshape(n_blk, EXPERT_BLOCK, hf.shape[-1])

    def expert_block(args):
        xb, e = args
        gu = xb @ w_gu[e] + b_gu[e]
        g, u = jnp.split(gu, 2, axis=-1)
        g = jnp.minimum(g, SWIGLU_LIMIT)
        u = jnp.clip(u, -SWIGLU_LIMIT, SWIGLU_LIMIT)
        act = g * jax.nn.sigmoid(SWIGLU_ALPHA * g) * (u + 1.0)
        return act @ w_dn[e] + b_dn[e]

    ys = lax.map(expert_block, (xs, blk_exp)).reshape(n_blk * EXPERT_BLOCK, -1)
    w_sorted = gates.reshape(n_assign)[order].astype(ys.dtype)
    out = jnp.zeros((t, ys.shape[-1]), ys.dtype).at[tok].add(ys[dest] * w_sorted[:, None])
    return out.reshape(h.shape[:-1] + (ys.shape[-1],))


def trunk_layer(x, cond, p, rope=None, ctx_cache=None, ssm_init=None):
    bsz, n, _ = x.shape
    mod = jax.nn.silu(cond) @ p["w_ada"] + p["b_ada"]
    sh1, sc1, g1, sh2, sc2, g2 = jnp.split(mod[:, None, :], 6, axis=-1)
    h = rmsnorm(x, p["norm1"]) * (1.0 + sc1) + sh1
    z_hy, z_q, z_kv, z_kr, z_s5 = jnp.split(h @ p["w_in"], IN_SPLITS, axis=-1)

    y_hy = hyena_mixer(z_hy, p)

    q = (rmsnorm(z_q, p["q_norm"]) @ p["w_uq"]).reshape(bsz, n, MLA_HEADS, QK_NOPE + QK_ROPE)
    q_nope, q_rope = q[..., :QK_NOPE], q[..., QK_NOPE:]
    c_kv = rmsnorm(z_kv, p["kv_norm"])
    k_rope = z_kr
    if rope is None:
        all_ckv, k_rope_all = c_kv, k_rope
    else:
        cos, sin = rope
        q_rope = apply_axial_rope(q_rope, cos[:, None], sin[:, None])
        ctx_ckv, ctx_kr = ctx_cache
        all_ckv = jnp.concatenate([c_kv, ctx_ckv.astype(c_kv.dtype)], axis=1)
        k_rope_all = jnp.concatenate([apply_axial_rope(k_rope, cos, sin),
                                      ctx_kr.astype(k_rope.dtype)], axis=1)
    lk = all_ckv.shape[1]
    k_nope = (all_ckv @ p["w_uk"]).reshape(bsz, lk, MLA_HEADS, QK_NOPE)
    v = (all_ckv @ p["w_uv"]).reshape(bsz, lk, MLA_HEADS, V_DIM)
    y_att = mla_attend(q_nope, q_rope, k_nope, k_rope_all, v)

    if ssm_init is None:
        zero = jnp.zeros((bsz, S5_GROUPS, S5_STATE), jnp.complex64)
        ssm_init = (zero, zero)
    y_s5, s_f, s_b = s5_mixer(z_s5, ssm_init[0], ssm_init[1], p)

    mixed = jnp.concatenate([y_hy.astype(h.dtype), y_att.astype(h.dtype), y_s5], axis=-1) @ p["w_out"]
    x = x + g1 * mixed
    h2 = rmsnorm(x, p["norm2"]) * (1.0 + sc2) + sh2
    x = x + g2 * routed_ffn(h2, p["w_router"], p["b_router"], p["w_gate_up"], p["b_gate_up"],
                            p["w_down"], p["b_down"])
    return x, (c_kv, k_rope, s_f, s_b)


def setup_inputs(seed: int = 0) -> dict:
    key = jax.random.key(seed)
    keys = iter(jax.random.split(key, 64))
    f32 = jnp.float32

    def nrm(shape, scale):
        return jax.random.normal(next(keys), shape, f32) * scale

    def gain(shape):
        return 1.0 + nrm(shape, 0.01)

    n_filt = 2 * HY_ORDER * HY_W
    decay_base = abs(math.log(DECAY_TARGET)) / jnp.linspace(FAST_DECAY_PCT, SLOW_DECAY_PCT, n_filt, dtype=f32)
    lam_im_base = math.pi * jnp.arange(S5_STATE, dtype=f32)
    ssm_shape = (DEPTH, 2, S5_GROUPS, S5_STATE)
    return {
        "x_prompt": nrm((BATCH, SEQ, D_MODEL), 1.0),
        "x_sample": nrm((DEC_BATCH, DEC_SEQ, D_MODEL), 1.0),
        "cache_ckv": nrm((DEC_BATCH, DEPTH, PAST_LEN, KV_RANK), 1.0),
        "cache_krope": nrm((DEC_BATCH, DEPTH, PAST_LEN, QK_ROPE), 1.0),
        "state_ssm": nrm((DEC_BATCH, DEPTH, 2, 2, S5_GROUPS, S5_STATE), 0.1),
        "c": nrm((DEC_BATCH, D_MODEL), 1.0),
        "c_ctx": nrm((D_MODEL,), 1.0),
        "w_ada": nrm((DEPTH, D_MODEL, 6 * D_MODEL), 0.5 * D_MODEL ** -0.5),
        "b_ada": nrm((DEPTH, 6 * D_MODEL), 0.01),
        "norm1": gain((DEPTH, D_MODEL)),
        "norm2": gain((DEPTH, D_MODEL)),
        "w_in": nrm((DEPTH, D_MODEL, IN_W), D_MODEL ** -0.5),
        "w_out": nrm((DEPTH, MIX_W, D_MODEL), MIX_W ** -0.5),
        "hy_conv_w": nrm((DEPTH, 3, HY_PROJ), 3 ** -0.5),
        "hy_conv_b": nrm((DEPTH, HY_PROJ), 0.01),
        "hy_fw1": nrm((DEPTH, POS_EMB, FILT_HID), POS_EMB ** -0.5),
        "hy_fb1": nrm((DEPTH, FILT_HID), 0.1),
        "hy_fw2": nrm((DEPTH, FILT_HID, FILT_HID), FILT_HID ** -0.5),
        "hy_fb2": nrm((DEPTH, FILT_HID), 0.1),
        "hy_freq": gain((DEPTH, 2, FILT_HID)),
        "hy_fw3": nrm((DEPTH, FILT_HID, n_filt), FILTER_OUT_SCALE * FILT_HID ** -0.5),
        "hy_decay": decay_base * gain((DEPTH, n_filt)),
        "hy_skip": nrm((DEPTH, HY_ORDER, HY_W), 0.5),
        "q_norm": gain((DEPTH, Q_RANK)),
        "kv_norm": gain((DEPTH, KV_RANK)),
        "w_uq": nrm((DEPTH, Q_RANK, MLA_HEADS * (QK_NOPE + QK_ROPE)), Q_RANK ** -0.5),
        "w_uk": nrm((DEPTH, KV_RANK, MLA_HEADS * QK_NOPE), KV_RANK ** -0.5),
        "w_uv": nrm((DEPTH, KV_RANK, MLA_HEADS * V_DIM), KV_RANK ** -0.5),
        "s5_lam_re": -0.5 + nrm(ssm_shape, 0.01),
        "s5_lam_im": lam_im_base + nrm(ssm_shape, 0.01),
        "s5_log_dt": jax.random.uniform(next(keys), (DEPTH, 2, S5_GROUPS), f32,
                                        math.log(DT_MIN), math.log(DT_MAX)),
        "s5_b_re": nrm((DEPTH, 2, S5_GROUPS, S5_STATE, S5_GROUP_CH), (2 * S5_GROUP_CH) ** -0.5),
        "s5_b_im": nrm((DEPTH, 2, S5_GROUPS, S5_STATE, S5_GROUP_CH), (2 * S5_GROUP_CH) ** -0.5),
        "s5_c_re": nrm((DEPTH, 2, S5_GROUPS, S5_GROUP_CH, S5_STATE), 0.35),
        "s5_c_im": nrm((DEPTH, 2, S5_GROUPS, S5_GROUP_CH, S5_STATE), 0.35),
        "s5_d": nrm((DEPTH, S5_W), 1.0),
        "s5_w_glu": nrm((DEPTH, S5_W, S5_W), S5_W ** -0.5),
        "s5_b_glu": nrm((DEPTH, S5_W), 0.01),
        "w_router": nrm((DEPTH, D_MODEL, N_EXPERTS), D_MODEL ** -0.5),
        "b_router": nrm((DEPTH, N_EXPERTS), 0.01),
        "w_gate_up": nrm((DEPTH, N_EXPERTS, D_MODEL, 2 * D_EXPERT), D_MODEL ** -0.5),
        "b_gate_up": nrm((DEPTH, N_EXPERTS, 2 * D_EXPERT), 0.01),
        "w_down": nrm((DEPTH, N_EXPERTS, D_EXPERT, D_MODEL), D_EXPERT ** -0.5),
        "b_down": nrm((DEPTH, N_EXPERTS, D_MODEL), 0.01),
        "norm_f": gain((D_MODEL,)),
    }


def reference(x_prompt, x_sample, cache_ckv, cache_krope, state_ssm, c, c_ctx,
              w_ada, b_ada, norm1, norm2, w_in, w_out,
              hy_conv_w, hy_conv_b, hy_fw1, hy_fb1, hy_fw2, hy_fb2, hy_freq, hy_fw3, hy_decay, hy_skip,
              q_norm, kv_norm, w_uq, w_uk, w_uv,
              s5_lam_re, s5_lam_im, s5_log_dt, s5_b_re, s5_b_im, s5_c_re, s5_c_im, s5_d,
              s5_w_glu, s5_b_glu,
              w_router, b_router, w_gate_up, b_gate_up, w_down, b_down, norm_f):
    f32 = jnp.float32
    rope = axial_rope_tables(x_sample.shape[1])
    xp, xs = x_prompt, x_sample
    ckv_out, kr_out, ssm_out = [], [], []
    for l in range(DEPTH):
        p = dict(
            w_ada=w_ada[l], b_ada=b_ada[l], norm1=norm1[l], norm2=norm2[l],
            w_in=w_in[l], w_out=w_out[l],
            hy_conv_w=hy_conv_w[l], hy_conv_b=hy_conv_b[l], hy_fw1=hy_fw1[l], hy_fb1=hy_fb1[l],
            hy_fw2=hy_fw2[l], hy_fb2=hy_fb2[l], hy_freq=hy_freq[l], hy_fw3=hy_fw3[l],
            hy_decay=hy_decay[l], hy_skip=hy_skip[l],
            q_norm=q_norm[l], kv_norm=kv_norm[l], w_uq=w_uq[l], w_uk=w_uk[l], w_uv=w_uv[l],
            s5_lam_re=s5_lam_re[l], s5_lam_im=s5_lam_im[l], s5_log_dt=s5_log_dt[l],
            s5_b_re=s5_b_re[l], s5_b_im=s5_b_im[l], s5_c_re=s5_c_re[l], s5_c_im=s5_c_im[l],
            s5_d=s5_d[l], s5_w_glu=s5_w_glu[l], s5_b_glu=s5_b_glu[l],
            w_router=w_router[l], b_router=b_router[l], w_gate_up=w_gate_up[l],
            b_gate_up=b_gate_up[l], w_down=w_down[l], b_down=b_down[l],
        )
        xp, (ckv, kr, s_f, s_b) = trunk_layer(xp, c_ctx[None, :], p)
        ckv_out.append(ckv)
        kr_out.append(kr)
        s = jnp.stack([s_f, s_b], axis=1)
        ssm_out.append(jnp.stack([jnp.real(s), jnp.imag(s)], axis=2))
        st = state_ssm[:, l].astype(f32)
        x0_f = lax.complex(st[:, 0, 0], st[:, 0, 1])
        x0_b = lax.complex(st[:, 1, 0], st[:, 1, 1])
        xs, _ = trunk_layer(xs, c, p, rope=rope,
                            ctx_cache=(cache_ckv[:, l], cache_krope[:, l]),
                            ssm_init=(x0_f, x0_b))
    y_prompt = rmsnorm(xp, norm_f)
    y_sample = rmsnorm(xs, norm_f)
    new_cache_ckv = jnp.stack(ckv_out, axis=1)
    new_cache_krope = jnp.stack(kr_out, axis=1)
    new_state_ssm = jnp.stack(ssm_out, axis=1)
    return (y_prompt, y_sample, new_cache_ckv, new_cache_krope, new_state_ssm)
```

```python
import functools
import math

import numpy as np
import jax
import jax.numpy as jnp
from jax import lax
from jax.experimental import pallas as pl
from jax.experimental.pallas import tpu as pltpu

F32 = jnp.float32
BF16 = jnp.bfloat16
I32 = jnp.int32
U32 = jnp.uint32
HIGHEST = lax.Precision.HIGHEST

D_MODEL = 2048
BATCH = 32
SEQ = 256
DEPTH = 4
DEC_BATCH = 4
DEC_SEQ = 1024
PAST_LEN = 512
GRID_W = 64
EPS = 1e-6
HY_W = 512
HY_PROJ = 3 * HY_W
POS_BANDS = 16
POS_EMB = 2 * POS_BANDS + 1
FILT_HID = 64
MLA_HEADS = 8
Q_RANK = 512
KV_RANK = 256
QK_NOPE = 128
QK_ROPE = 64
V_DIM = 128
MLA_W = MLA_HEADS * V_DIM
ROPE_THETA = 10000.0
S5_W = 512
S5_GROUP_CH = 16
S5_GROUPS = 32
S5_STATE = 64
S5_NS = S5_GROUPS * S5_STATE
N_EXPERTS = 32
TOP_K = 4
D_EXPERT = D_MODEL
SWIGLU_LIMIT = 7.0
SWIGLU_ALPHA = 1.702

T_P = BATCH * SEQ
T_S = DEC_BATCH * DEC_SEQ
T_ALL = T_P + T_S
N_MOD = 8
QH = 256
KVR_W = KV_RANK + 2 * QK_ROPE
ROW_TILE = 256
TAB_ROWS = SEQ + DEC_SEQ

VMEM_LIMIT_BYTES = 56 * 1024 * 1024

EXPERT_ROWS = 512
N_ASSIGN = T_ALL * TOP_K
N_BLK = N_ASSIGN // EXPERT_ROWS + N_EXPERTS
N_SLOTS = N_BLK * EXPERT_ROWS
EXPERT_TN = 512
COMBINE_ROWS = 256


def _cparams(*sem):
    return pltpu.CompilerParams(dimension_semantics=sem, vmem_limit_bytes=VMEM_LIMIT_BYTES)


def _mod_row(i, tm):
    npb = T_P // tm
    sb = DEC_SEQ // tm
    return jnp.where(i < npb, 0, 1 + (i - npb) // sb)


def _tab_blk(i, tm):
    npb = T_P // tm
    pb = SEQ // tm
    sb = DEC_SEQ // tm
    return jnp.where(i < npb, i % pb, pb + (i - npb) % sb)


def _rms(x, g):
    return x * lax.rsqrt(jnp.mean(x * x, axis=-1, keepdims=True) + EPS) * g


def _bdot(a, b):
    return jnp.dot(a, b, preferred_element_type=F32)


def _ada_kernel(c_ref, w_ref, b_ref, o_ref):
    s = jax.nn.silu(c_ref[...]).astype(BF16)
    o_ref[...] = _bdot(s, w_ref[...].astype(BF16)) + b_ref[...]


def ada_modulation(cond8, w_ada, b_ada):
    tn = 1024
    n = 6 * D_MODEL
    return pl.pallas_call(
        _ada_kernel,
        grid=(DEPTH, n // tn),
        in_specs=[
            pl.BlockSpec((N_MOD, D_MODEL), lambda l, j: (0, 0)),
            pl.BlockSpec((None, D_MODEL, tn), lambda l, j: (l, 0, j)),
            pl.BlockSpec((None, 1, tn), lambda l, j: (l, 0, j)),
        ],
        out_specs=pl.BlockSpec((None, N_MOD, tn), lambda l, j: (l, 0, j)),
        out_shape=jax.ShapeDtypeStruct((DEPTH, N_MOD, n), F32),
        compiler_params=_cparams("arbitrary", "arbitrary"),
        name="ada_modulation",
    )(cond8, w_ada, b_ada.reshape(DEPTH, 1, n))


IN_COLS = (HY_PROJ, Q_RANK, KVR_W, S5_W)
IN_W_AUG = sum(IN_COLS)


def _in_proj_kernel(x_ref, g_ref, sc_ref, sh_ref, w_ref, zhy_ref, zq_ref, zkvr_ref, zs5_ref):
    h = _rms(x_ref[...], g_ref[...]) * (1.0 + sc_ref[...]) + sh_ref[...]
    h = h.astype(BF16)
    off = 0
    for o_ref, wd in zip((zhy_ref, zq_ref, zkvr_ref, zs5_ref), IN_COLS):
        o_ref[...] = _bdot(h, w_ref[:, off:off + wd])
        off += wd


def in_proj(x, g, sc, sh, w_aug):
    tm = ROW_TILE
    t = x.shape[0]
    mod_spec = pl.BlockSpec((None, 1, D_MODEL), lambda i: (_mod_row(i, tm), 0, 0))
    return pl.pallas_call(
        _in_proj_kernel,
        grid=(t // tm,),
        in_specs=[
            pl.BlockSpec((tm, D_MODEL), lambda i: (i, 0)),
            pl.BlockSpec((1, D_MODEL), lambda i: (0, 0)),
            mod_spec, mod_spec,
            pl.BlockSpec((D_MODEL, IN_W_AUG), lambda i: (0, 0)),
        ],
        out_specs=[pl.BlockSpec((tm, wd), lambda i: (i, 0)) for wd in IN_COLS],
        out_shape=[jax.ShapeDtypeStruct((t, wd), F32) for wd in IN_COLS],
        compiler_params=_cparams("arbitrary"),
        name="in_proj",
    )(x, g, sc, sh, w_aug)


def _dft_tables(n):
    f = np.arange(n)[:, None]
    s = np.arange(n)[None, :]
    ang = (np.pi / n) * ((f * s) % (2 * n)).astype(np.float64)
    cos = np.cos(ang)
    sin = np.sin(ang)
    nyq = np.where(np.arange(n) % 2 == 0, 1.0, -1.0)
    fs = sin.copy()
    fs[0, :] = nyq
    fwd = np.concatenate([cos, fs], axis=0)
    wgt = np.full((n,), 2.0)
    wgt[0] = 1.0
    gc = (cos * wgt[:, None]).T / (2 * n)
    gs = (sin * wgt[:, None]).T / (2 * n)
    gs[:, 0] = nyq / (2 * n)
    inv = np.concatenate([gc, gs], axis=1)
    return fwd, inv


def _split_bf16(a):
    hi = jnp.asarray(a, F32).astype(BF16)
    lo = (jnp.asarray(a, F32) - hi.astype(F32)).astype(BF16)
    return hi, lo


def _hyena_spec_kernel(n, feat_ref, t_ref, fw1_ref, fb1_ref, fw2_ref, fb2_ref, freq_ref,
                       w3f0_ref, w3f1_ref, w3b0_ref, w3b1_ref,
                       dcf0_ref, dcf1_ref, dcb0_ref, dcb1_ref,
                       fhi_ref, flo_ref, o_ref):
    hdot = functools.partial(jnp.dot, precision=HIGHEST, preferred_element_type=F32)
    h = jnp.sin(freq_ref[0:1, :] * (hdot(feat_ref[...], fw1_ref[...]) + fb1_ref[...]))
    h = jnp.sin(freq_ref[1:2, :] * (hdot(h, fw2_ref[...]) + fb2_ref[...]))
    tcol = t_ref[...]
    row0 = lax.broadcasted_iota(I32, (n, 1), 0) == 0

    def filt(w3_ref, dc_ref):
        return hdot(h, w3_ref[...]) * jnp.exp(-tcol * jnp.abs(dc_ref[...]))

    def dft(fpart_hi, fpart_lo, k):
        k_hi = k.astype(BF16)
        k_lo = (k - k_hi.astype(F32)).astype(BF16)
        return _bdot(fpart_hi, k_hi) + (_bdot(fpart_hi, k_lo) + _bdot(fpart_lo, k_hi))

    for o, (wf, wb, df, db) in enumerate(((w3f0_ref, w3b0_ref, dcf0_ref, dcb0_ref),
                                          (w3f1_ref, w3b1_ref, dcf1_ref, dcb1_ref))):
        fwd = filt(wf, df)
        bwd = jnp.where(row0, 0.0, filt(wb, db))
        ks = dft(fhi_ref[...], flo_ref[...], fwd + bwd)
        kd = dft(fhi_ref[n:, :], flo_ref[n:, :], fwd - bwd)
        kr = ks[:n]
        ksp = jnp.where(row0, ks[n:n + 1], kd)
        o_ref[3 * o + 0] = kr
        o_ref[3 * o + 1] = jnp.where(row0, 0.0, ksp)
        o_ref[3 * o + 2] = jnp.where(row0, ksp, kr)


def hyena_spectra(n, fw1p, fb1, fw2, fb2, freq, fw3, decay, fwd_hi, fwd_lo):
    tc = 256
    nc = HY_W // tc
    f32 = F32
    t = jnp.linspace(0.0, 1.0, n, dtype=f32)[:, None]
    w = (2.0 * math.pi / n) * jnp.arange(n, dtype=f32)[:, None]
    bands = jnp.linspace(1e-4, POS_BANDS - 1, POS_BANDS, dtype=f32)
    feat = jnp.concatenate([t, jnp.cos(w * bands), -jnp.sin(w * bands),
                            jnp.zeros((n, FILT_HID - POS_EMB), f32)], axis=-1)

    def w3_spec(d, o):
        return pl.BlockSpec((None, FILT_HID, tc), lambda l, c: (l, 0, (d * 2 + o) * nc + c))

    def dc_spec(d, o):
        return pl.BlockSpec((None, 1, tc), lambda l, c: (l, 0, (d * 2 + o) * nc + c))

    const2 = lambda shape: pl.BlockSpec(shape, lambda l, c: (0, 0))
    lay3 = lambda a, b: pl.BlockSpec((None, a, b), lambda l, c: (l, 0, 0))
    return pl.pallas_call(
        functools.partial(_hyena_spec_kernel, n),
        grid=(DEPTH, nc),
        in_specs=[
            const2((n, FILT_HID)), const2((n, 1)),
            lay3(FILT_HID, FILT_HID), lay3(1, FILT_HID), lay3(FILT_HID, FILT_HID), lay3(1, FILT_HID),
            lay3(2, FILT_HID),
            w3_spec(0, 0), w3_spec(0, 1), w3_spec(1, 0), w3_spec(1, 1),
            dc_spec(0, 0), dc_spec(0, 1), dc_spec(1, 0), dc_spec(1, 1),
            const2((2 * n, n)), const2((2 * n, n)),
        ],
        out_specs=pl.BlockSpec((None, 6, n, tc), lambda l, c: (l, 0, 0, c)),
        out_shape=jax.ShapeDtypeStruct((DEPTH, 6, n, HY_W), F32),
        compiler_params=_cparams("arbitrary", "arbitrary"),
        name=f"hyena_spectra_{n}",
    )(feat, t, fw1p, fb1, fw2, fb2, freq, fw3, fw3, fw3, fw3, decay, decay, decay, decay,
      fwd_hi, fwd_lo)


def _hyena_kernel(n, nseq, zv_ref, z1_ref, z2_ref, wv_ref, w1_ref, w2_ref, bv_ref, b1_ref, b2_ref,
                  f_ref, g_ref, spec_ref, skip_ref, *rest):
    o_ref = rest[-1]
    row = lax.broadcasted_iota(I32, (n, 1), 0)
    first = row == 0
    last = row == n - 1

    def conv3(z, w_ref, b_ref):
        zm = jnp.where(first, 0.0, pltpu.roll(z, 1, 0))
        zp = jnp.where(last, 0.0, pltpu.roll(z, n - 1, 0))
        return zm * w_ref[0:1, :] + z * w_ref[1:2, :] + zp * w_ref[2:3, :] + b_ref[...]

    def long_conv(u, o):
        uu = _bdot(f_ref[...], u.astype(BF16))
        a = uu[:n]
        b = uu[n:]
        kr = spec_ref[3 * o + 0]
        ks = spec_ref[3 * o + 1]
        dd = spec_ref[3 * o + 2]
        yre = (kr * a - ks * b).astype(BF16)
        zz = (dd * b + ks * a).astype(BF16)
        y = _bdot(g_ref[:, :n], yre) + _bdot(g_ref[:, n:], zz)
        return y + u * skip_ref[o:o + 1, :]

    for s in range(nseq):
        rows = pl.ds(s * n, n)
        v = conv3(zv_ref[rows, :], wv_ref, bv_ref)
        x1 = conv3(z1_ref[rows, :], w1_ref, b1_ref)
        x2 = conv3(z2_ref[rows, :], w2_ref, b2_ref)
        y = x1 * long_conv(v, 0)
        y = x2 * long_conv(y, 1)
        o_ref[rows, :] = y.astype(o_ref.dtype)


def hyena(z_hy, conv_w, conv_b, f_hi, g_inv, spec, skip, *, n, nbatch, row0, nseq, t_out, prev=None):
    tc = 256
    nc = HY_W // tc
    rb = nseq * n
    blk0 = row0 // rb
    assert row0 % rb == 0 and nbatch % nseq == 0

    def zspec(part):
        return pl.BlockSpec((rb, tc), lambda c, b: (blk0 + b, part * nc + c))

    def wspec(part, rows):
        return pl.BlockSpec((rows, tc), lambda c, b: (0, part * nc + c))

    in_specs = [zspec(0), zspec(1), zspec(2), wspec(0, 3), wspec(1, 3), wspec(2, 3),
                wspec(0, 1), wspec(1, 1), wspec(2, 1),
                pl.BlockSpec((2 * n, n), lambda c, b: (0, 0)),
                pl.BlockSpec((n, 2 * n), lambda c, b: (0, 0)),
                pl.BlockSpec((6, n, tc), lambda c, b: (0, 0, c)),
                pl.BlockSpec((2, tc), lambda c, b: (0, c))]
    args = [z_hy, z_hy, z_hy, conv_w, conv_w, conv_w, conv_b, conv_b, conv_b, f_hi, g_inv, spec, skip]
    aliases = {}
    if prev is not None:
        in_specs.append(pl.BlockSpec(memory_space=pl.ANY))
        args.append(prev)
        aliases = {len(args) - 1: 0}
    return pl.pallas_call(
        functools.partial(_hyena_kernel, n, nseq),
        grid=(nc, nbatch // nseq),
        in_specs=in_specs,
        out_specs=pl.BlockSpec((rb, tc), lambda c, b: (blk0 + b, c)),
        out_shape=jax.ShapeDtypeStruct((t_out, HY_W), BF16),
        input_output_aliases=aliases,
        compiler_params=_cparams("arbitrary", "arbitrary"),
        name=f"hyena_{n}",
    )(*args)


def _q_proj_kernel(z_ref, g_ref, w_ref, tab_ref, o_ref):
    h = _rms(z_ref[...], g_ref[...]).astype(BF16)
    tab = tab_ref[...]
    for hd in range(MLA_HEADS):
        cols = slice(hd * QH, (hd + 1) * QH)
        o_ref[:, cols] = (_bdot(h, w_ref[:, cols]) * tab).astype(BF16)


def q_proj(z_q, g, w_q, tab_q):
    tm = ROW_TILE
    t = z_q.shape[0]
    return pl.pallas_call(
        _q_proj_kernel,
        grid=(t // tm,),
        in_specs=[
            pl.BlockSpec((tm, Q_RANK), lambda i: (i, 0)),
            pl.BlockSpec((1, Q_RANK), lambda i: (0, 0)),
            pl.BlockSpec((Q_RANK, MLA_HEADS * QH), lambda i: (0, 0)),
            pl.BlockSpec((tm, QH), lambda i: (_tab_blk(i, tm), 0)),
        ],
        out_specs=pl.BlockSpec((tm, MLA_HEADS * QH), lambda i: (i, 0)),
        out_shape=jax.ShapeDtypeStruct((t, MLA_HEADS * QH), BF16),
        compiler_params=_cparams("arbitrary"),
        name="q_proj",
    )(z_q, g, w_q, tab_q)


def _kv_prep_kernel(z_ref, g_ref, tab_ref, o_ref):
    o_ref[:, :KV_RANK] = _rms(z_ref[:, :KV_RANK], g_ref[...])
    t = z_ref[:, KV_RANK:] * tab_ref[...]
    o_ref[:, KV_RANK:] = t + pltpu.roll(t, QK_ROPE, 1)


def kv_prep(z_kvr, g, tab_k):
    tm = ROW_TILE
    t = z_kvr.shape[0]
    return pl.pallas_call(
        _kv_prep_kernel,
        grid=(t // tm,),
        in_specs=[
            pl.BlockSpec((tm, KVR_W), lambda i: (i, 0)),
            pl.BlockSpec((1, KV_RANK), lambda i: (0, 0)),
            pl.BlockSpec((tm, 2 * QK_ROPE), lambda i: (_tab_blk(i, tm), 0)),
        ],
        out_specs=pl.BlockSpec((tm, KVR_W), lambda i: (i, 0)),
        out_shape=jax.ShapeDtypeStruct((t, KVR_W), F32),
        compiler_params=_cparams("arbitrary"),
        name="kv_prep",
    )(z_kvr, g, tab_k)


def _kv_up_kernel(a_ref, w_ref, k_ref, v_ref):
    a = a_ref[...].astype(BF16)
    nk = MLA_HEADS * QH
    k_ref[...] = _bdot(a, w_ref[:, :nk]).astype(BF16)
    v_ref[...] = _bdot(a, w_ref[:, nk:]).astype(BF16)


def kv_up(a, w_kv):
    tm = 512
    t = a.shape[0]
    nk = MLA_HEADS * QH
    return pl.pallas_call(
        _kv_up_kernel,
        grid=(t // tm,),
        in_specs=[
            pl.BlockSpec((tm, KVR_W), lambda i: (i, 0)),
            pl.BlockSpec((KVR_W, nk + MLA_W), lambda i: (0, 0)),
        ],
        out_specs=[pl.BlockSpec((tm, nk), lambda i: (i, 0)),
                   pl.BlockSpec((tm, MLA_W), lambda i: (i, 0))],
        out_shape=[jax.ShapeDtypeStruct((t, nk), BF16), jax.ShapeDtypeStruct((t, MLA_W), BF16)],
        compiler_params=_cparams("arbitrary"),
        name="kv_up",
    )(a, w_kv)


def _attn_kernel(q_ref, k_ref, v_ref, *rest):
    o_ref = rest[-1]
    for hd in range(MLA_HEADS):
        q = q_ref[:, hd * QH:(hd + 1) * QH]
        k = k_ref[:, hd * QH:(hd + 1) * QH]
        s = lax.dot_general(q, k, (((1,), (1,)), ((), ())), preferred_element_type=F32)
        m = jnp.max(s, axis=-1, keepdims=True)
        p = jnp.exp(s - m)
        l = jnp.sum(p, axis=-1, keepdims=True)
        o = _bdot(p.astype(BF16), v_ref[:, hd * V_DIM:(hd + 1) * V_DIM])
        o_ref[:, hd * V_DIM:(hd + 1) * V_DIM] = (o / l).astype(o_ref.dtype)


def attention(q, k, v, *, nbatch, lq, lk, row0, t_out, prev=None):
    tq = ROW_TILE
    nq = lq // tq
    blk0 = row0 // tq
    in_specs = [
        pl.BlockSpec((tq, MLA_HEADS * QH), lambda b, i: (blk0 + b * nq + i, 0)),
        pl.BlockSpec((lk, MLA_HEADS * QH), lambda b, i: (b, 0)),
        pl.BlockSpec((lk, MLA_W), lambda b, i: (b, 0)),
    ]
    args = [q, k, v]
    aliases = {}
    if prev is not None:
        in_specs.append(pl.BlockSpec(memory_space=pl.ANY))
        args.append(prev)
        aliases = {3: 0}
    return pl.pallas_call(
        _attn_kernel,
        grid=(nbatch, nq),
        in_specs=in_specs,
        out_specs=pl.BlockSpec((tq, MLA_W), lambda b, i: (blk0 + b * nq + i, 0)),
        out_shape=jax.ShapeDtypeStruct((t_out, MLA_W), BF16),
        input_output_aliases=aliases,
        compiler_params=_cparams("arbitrary", "arbitrary"),
        name=f"attention_{lk}",
    )(*args)


def _s5_param_kernel(lr_ref, li_ref, dt_ref, lrx_ref, lix_ref, dtx_ref, bre_ref, bim_ref,
                     lbr_ref, lbi_ref, bbr_ref, bbi_ref):
    def lam_bar(lr, li, ldt):
        lr = jnp.minimum(lr, -1e-4)
        dt = jnp.exp(ldt)
        e = jnp.exp(lr * dt)
        return lr, e * jnp.cos(li * dt), e * jnp.sin(li * dt)

    _, lbr, lbi = lam_bar(lr_ref[...], li_ref[...], dt_ref[...])
    lbr_ref[...] = lbr
    lbi_ref[...] = lbi
    lr, xr, xi = lam_bar(lrx_ref[...], lix_ref[...], dtx_ref[...])
    li = lix_ref[...]
    nr = xr - 1.0
    den = lr * lr + li * li
    cr = (nr * lr + xi * li) / den
    ci = (xi * lr - nr * li) / den
    bbr_ref[...] = cr * bre_ref[...] - ci * bim_ref[...]
    bbi_ref[...] = cr * bim_ref[...] + ci * bre_ref[...]


def s5_params(lam_re, lam_im, log_dt, b_re, b_im):
    r = DEPTH * 2 * S5_GROUPS
    ldt = jnp.broadcast_to(log_dt[..., None], lam_re.shape)
    small = [a.reshape(r, S5_STATE) for a in (lam_re, lam_im, ldt)]
    wide = [jnp.repeat(a, S5_GROUP_CH, axis=-1) for a in small]
    bs = [a.reshape(r, S5_STATE * S5_GROUP_CH) for a in (b_re, b_im)]
    outs = pl.pallas_call(
        _s5_param_kernel,
        out_shape=[jax.ShapeDtypeStruct((r, S5_STATE), F32)] * 2
        + [jax.ShapeDtypeStruct((r, S5_STATE * S5_GROUP_CH), F32)] * 2,
        name="s5_params",
    )(*small, *wide, *bs)
    lbr, lbi, bbr, bbi = outs
    shp = (DEPTH, 2, S5_GROUPS, S5_STATE)
    return (lbr.reshape(shp), lbi.reshape(shp),
            bbr.reshape(shp + (S5_GROUP_CH,)), bbi.reshape(shp + (S5_GROUP_CH,)))


S5_SUB = 8
S5_LANES = 512


def _s5_scan_kernel(tq, u_ref, bre_ref, bim_ref, cre_ref, cim_ref, lam_ref, x0_ref,
                    y_ref, xf_ref, s_ref, x_ref):
    d = pl.program_id(0)
    j = pl.program_id(2)
    rows = tq * S5_SUB

    @pl.when(j == 0)
    def _():
        x_ref[...] = x0_ref[...]

    u = u_ref[...].reshape(rows, S5_W).astype(BF16)
    s_ref[:, :S5_NS] = _bdot(u, bre_ref[...])
    s_ref[:, S5_NS:] = _bdot(u, bim_ref[...])

    for c in range(S5_NS // S5_LANES):
        re = pl.ds(c * S5_LANES, S5_LANES)
        im = pl.ds(S5_NS + c * S5_LANES, S5_LANES)
        lr = lam_ref[:, re]
        li = lam_ref[:, im]

        def step(i, carry):
            xr, xi = carry
            t = jnp.where(d == 0, i, tq - 1 - i)
            r = pl.ds(pl.multiple_of(t * S5_SUB, S5_SUB), S5_SUB)
            nr = lr * xr - li * xi + s_ref[r, re]
            ni = lr * xi + li * xr + s_ref[r, im]
            s_ref[r, re] = nr
            s_ref[r, im] = ni
            return nr, ni

        xr, xi = lax.fori_loop(0, tq, step, (x_ref[:, re], x_ref[:, im]))
        x_ref[:, re] = xr
        x_ref[:, im] = xi

    y = (_bdot(s_ref[:, :S5_NS].astype(BF16), cre_ref[...])
         - _bdot(s_ref[:, S5_NS:].astype(BF16), cim_ref[...]))
    y_ref[...] = y.reshape(tq, S5_SUB, S5_W)

    @pl.when(j == pl.num_programs(2) - 1)
    def _():
        xf_ref[...] = x_ref[...]


def s5_scan(u_tm, b_re, b_im, c_re, c_im, lam8, x0):
    n, bp, _ = u_tm.shape
    tq = 64
    nch = n // tq

    def tci(d, j):
        return d * (nch - 1) + (1 - 2 * d) * j

    dir3 = lambda a, b: pl.BlockSpec((None, a, b), lambda d, g, j: (d, 0, 0))
    return pl.pallas_call(
        functools.partial(_s5_scan_kernel, tq),
        grid=(2, bp // S5_SUB, nch),
        in_specs=[
            pl.BlockSpec((tq, S5_SUB, S5_W), lambda d, g, j: (tci(d, j), g, 0)),
            dir3(S5_W, S5_NS), dir3(S5_W, S5_NS), dir3(S5_NS, S5_W), dir3(S5_NS, S5_W),
            dir3(S5_SUB, 2 * S5_NS),
            pl.BlockSpec((None, S5_SUB, 2 * S5_NS), lambda d, g, j: (d, g, 0)),
        ],
        out_specs=[
            pl.BlockSpec((None, tq, S5_SUB, S5_W), lambda d, g, j: (d, tci(d, j), g, 0)),
            pl.BlockSpec((None, S5_SUB, 2 * S5_NS), lambda d, g, j: (d, g, 0)),
        ],
        out_shape=[jax.ShapeDtypeStruct((2, n, bp, S5_W), F32),
                   jax.ShapeDtypeStruct((2, bp, 2 * S5_NS), F32)],
        scratch_shapes=[pltpu.VMEM((tq * S5_SUB, 2 * S5_NS), F32),
                        pltpu.VMEM((S5_SUB, 2 * S5_NS), F32)],
        compiler_params=_cparams("arbitrary", "arbitrary", "arbitrary"),
        name=f"s5_scan_{n}",
    )(u_tm, b_re, b_im, c_re, c_im, lam8, x0)


def _s5_glu_kernel(y_ref, u_ref, d_ref, w_ref, b_ref, o_ref):
    y = y_ref[0] + y_ref[1] + u_ref[...] * d_ref[...]
    zg = jax.nn.gelu(y)
    gate = jax.nn.sigmoid(_bdot(zg.astype(BF16), w_ref[...]) + b_ref[...])
    o_ref[...] = (zg * gate).astype(o_ref.dtype)


def s5_glu(y2, u, d, w_glu, b_glu):
    tm = 512
    r = u.shape[0]
    return pl.pallas_call(
        _s5_glu_kernel,
        grid=(r // tm,),
        in_specs=[
            pl.BlockSpec((2, tm, S5_W), lambda i: (0, i, 0)),
            pl.BlockSpec((tm, S5_W), lambda i: (i, 0)),
            pl.BlockSpec((1, S5_W), lambda i: (0, 0)),
            pl.BlockSpec((S5_W, S5_W), lambda i: (0, 0)),
            pl.BlockSpec((1, S5_W), lambda i: (0, 0)),
        ],
        out_specs=pl.BlockSpec((tm, S5_W), lambda i: (i, 0)),
        out_shape=jax.ShapeDtypeStruct((r, S5_W), BF16),
        compiler_params=_cparams("arbitrary"),
        name="s5_glu",
    )(y2, u, d, w_glu, b_glu)


def _out_proj_kernel(yh_ref, ya_ref, ys_ref, w_ref, x_ref, g1_ref, n2_ref, sc_ref, sh_ref,
                     wr_ref, br_ref, x1_ref, hp_ref, ti_ref, gt_ref):
    acc = _bdot(yh_ref[...], w_ref[:HY_W, :])
    acc += _bdot(ya_ref[...], w_ref[HY_W:HY_W + MLA_W, :])
    acc += _bdot(ys_ref[...], w_ref[HY_W + MLA_W:, :])
    x1 = x_ref[...] + g1_ref[...] * acc
    x1_ref[...] = x1
    h2 = _rms(x1, n2_ref[...]) * (1.0 + sc_ref[...]) + sh_ref[...]

    bits = lax.bitcast_convert_type(h2.astype(BF16).astype(F32), U32)
    half = D_MODEL // 2
    hp_ref[...] = (bits[:, half:] & jnp.uint32(0xFFFF0000)) | (bits[:, :half] >> 16)

    logits = jnp.dot(h2, wr_ref[...], precision=HIGHEST, preferred_element_type=F32) + br_ref[...]
    lane = lax.broadcasted_iota(I32, logits.shape, 1)
    neg = jnp.float32(-jnp.inf)
    logits = jnp.where(lane < N_EXPERTS, logits, neg)
    ti = jnp.zeros(logits.shape, I32)
    tv = jnp.zeros(logits.shape, F32)
    v0 = None
    for k in range(TOP_K):
        m = jnp.max(logits, axis=-1, keepdims=True)
        idx = jnp.min(jnp.where(logits == m, lane, 128), axis=-1, keepdims=True)
        if k == 0:
            v0 = m
        ti = jnp.where(lane == k, idx, ti)
        tv = jnp.where(lane == k, jnp.exp(m - v0), tv)
        logits = jnp.where(lane == idx, neg, logits)
    ti_ref[...] = ti
    gt_ref[...] = tv / jnp.sum(tv, axis=-1, keepdims=True)


def out_proj(y_hy, y_att, y_s5, w_out, x, g1, n2, sc2, sh2, w_r, b_r):
    tm = ROW_TILE
    t = x.shape[0]
    row = lambda wd: pl.BlockSpec((tm, wd), lambda i: (i, 0))
    const = lambda a, b: pl.BlockSpec((a, b), lambda i: (0, 0))
    mod_spec = pl.BlockSpec((None, 1, D_MODEL), lambda i: (_mod_row(i, tm), 0, 0))
    return pl.pallas_call(
        _out_proj_kernel,
        grid=(t // tm,),
        in_specs=[row(HY_W), row(MLA_W), row(S5_W), const(D_MODEL, D_MODEL), row(D_MODEL),
                  mod_spec, const(1, D_MODEL), mod_spec, mod_spec,
                  const(D_MODEL, 128), const(1, 128)],
        out_specs=[row(D_MODEL), row(D_MODEL // 2), row(128), row(128)],
        out_shape=[jax.ShapeDtypeStruct((t, D_MODEL), F32),
                   jax.ShapeDtypeStruct((t, D_MODEL // 2), U32),
                   jax.ShapeDtypeStruct((t, 128), I32),
                   jax.ShapeDtypeStruct((t, 128), F32)],
        compiler_params=_cparams("arbitrary"),
        name="out_proj",
    )(y_hy, y_att, y_s5, w_out, x, g1, n2, sc2, sh2, w_r, b_r)


def route_tables(top_i):
    rb = EXPERT_ROWS
    flat_e = top_i.reshape(N_ASSIGN)
    onehot = (flat_e[:, None] == jnp.arange(N_EXPERTS, dtype=I32)[None, :]).astype(I32)
    csum = jnp.cumsum(onehot, axis=0)
    counts = csum[-1]
    rank = jnp.take_along_axis(csum, flat_e[:, None], axis=1)[:, 0] - 1
    blk_cnt = (counts + rb - 1) // rb
    blk_end = jnp.cumsum(blk_cnt)
    blk_start = blk_end - blk_cnt
    total_blk = blk_end[-1]
    dest = blk_start[flat_e] * rb + rank
    tok = jnp.arange(N_ASSIGN, dtype=I32) // TOP_K
    slot_tok = jnp.zeros((N_SLOTS,), I32).at[dest].set(tok)

    nj = D_MODEL // EXPERT_TN
    s = jnp.arange(N_BLK * nj, dtype=I32)
    blk0 = s // nj
    valid = blk0 < total_blk
    last_e = jnp.minimum(jnp.searchsorted(blk_end, total_blk - 1, side="right"), N_EXPERTS - 1).astype(I32)
    e = jnp.minimum(jnp.searchsorted(blk_end, blk0, side="right"), N_EXPERTS - 1).astype(I32)
    c_e = jnp.maximum(blk_cnt[e], 1)
    local = s - blk_start[e] * nj
    j = local // c_e
    r = local % c_e
    steps = dict(
        x_blk=jnp.where(valid, blk_start[e] + r, blk0),
        o_j=jnp.where(valid, j, s % nj),
        w_e=jnp.where(valid, e, last_e),
        w_j=jnp.where(valid, j, nj - 1),
        first=(valid & (r == 0)).astype(I32),
        valid=valid.astype(I32),
    )
    return slot_tok, dest, steps


def _gather_rows_kernel(idx_ref, src_ref, o_ref, sem):
    nrows = o_ref.shape[0]

    def copy(r, tok):
        return pltpu.make_async_copy(src_ref.at[pl.ds(tok, 1)], o_ref.at[pl.ds(r, 1)], sem)

    def issue(r, c):
        copy(r, idx_ref[0, r]).start()
        return c

    def drain(r, c):
        copy(r, 0).wait()
        return c

    lax.fori_loop(0, nrows, issue, 0)
    lax.fori_loop(0, nrows, drain, 0)


def gather_rows(src, idx, rows_per_step):
    n = idx.shape[0]
    nb = n // rows_per_step
    w = src.shape[1]
    return pl.pallas_call(
        _gather_rows_kernel,
        grid=(nb,),
        in_specs=[
            pl.BlockSpec((None, 1, rows_per_step), lambda i: (i, 0, 0), memory_space=pltpu.SMEM),
            pl.BlockSpec(memory_space=pl.ANY),
        ],
        out_specs=pl.BlockSpec((rows_per_step, w), lambda i: (i, 0)),
        out_shape=jax.ShapeDtypeStruct((n, w), src.dtype),
        scratch_shapes=[pltpu.SemaphoreType.DMA(())],
        compiler_params=_cparams("arbitrary"),
        name="dispatch_gather",
    )(idx.reshape(nb, 1, rows_per_step), src)


def _unpack_bf16(words):
    lo = lax.bitcast_convert_type(words << 16, F32).astype(BF16)
    hi = lax.bitcast_convert_type(words & jnp.uint32(0xFFFF0000), F32).astype(BF16)
    return lo, hi


def _expert_up_kernel(xb_ref, oj_ref, we_ref, wj_ref, first_ref, valid_ref,
                      x_ref, wg_ref, wu_ref, bg_ref, bu_ref, o_ref, wgb_ref, wub_ref):
    s = pl.program_id(0)

    @pl.when(first_ref[s] == 1)
    def _():
        wgb_ref[...] = wg_ref[...].astype(BF16)
        wub_ref[...] = wu_ref[...].astype(BF16)

    @pl.when(valid_ref[s] == 1)
    def _():
        lo, hi = _unpack_bf16(x_ref[...])
        half = D_MODEL // 2
        g = _bdot(lo, wgb_ref[:half, :]) + _bdot(hi, wgb_ref[half:, :]) + bg_ref[...]
        u = _bdot(lo, wub_ref[:half, :]) + _bdot(hi, wub_ref[half:, :]) + bu_ref[...]
        g = jnp.minimum(g, SWIGLU_LIMIT)
        u = jnp.clip(u, -SWIGLU_LIMIT, SWIGLU_LIMIT)
        o_ref[...] = (g * jax.nn.sigmoid(SWIGLU_ALPHA * g) * (u + 1.0)).astype(o_ref.dtype)

    @pl.when(valid_ref[s] == 0)
    def _():
        o_ref[...] = jnp.zeros(o_ref.shape, o_ref.dtype)


def expert_up(layer, xs, w_gu, b_gu, steps):
    tn = EXPERT_TN
    nj = D_EXPERT // tn
    n_steps = N_BLK * nj
    wspec = lambda up: pl.BlockSpec(
        (None, None, D_MODEL, tn), lambda s, xb, oj, we, wj, fi, va: (layer, we[s], 0, up * nj + wj[s]))
    bspec = lambda up: pl.BlockSpec(
        (None, None, 1, tn), lambda s, xb, oj, we, wj, fi, va: (layer, we[s], 0, up * nj + wj[s]))
    return pl.pallas_call(
        _expert_up_kernel,
        grid_spec=pltpu.PrefetchScalarGridSpec(
            num_scalar_prefetch=6,
            grid=(n_steps,),
            in_specs=[
                pl.BlockSpec((EXPERT_ROWS, D_MODEL // 2), lambda s, xb, oj, we, wj, fi, va: (xb[s], 0)),
                wspec(0), wspec(1), bspec(0), bspec(1),
            ],
            out_specs=pl.BlockSpec((EXPERT_ROWS, tn), lambda s, xb, oj, we, wj, fi, va: (xb[s], oj[s])),
            scratch_shapes=[pltpu.VMEM((D_MODEL, tn), BF16), pltpu.VMEM((D_MODEL, tn), BF16)],
        ),
        out_shape=jax.ShapeDtypeStruct((N_SLOTS, D_EXPERT), BF16),
        compiler_params=_cparams("arbitrary"),
        name="expert_up",
    )(steps["x_blk"], steps["o_j"], steps["w_e"], steps["w_j"], steps["first"], steps["valid"],
      xs, w_gu, w_gu, b_gu, b_gu)


def _expert_down_kernel(xb_ref, oj_ref, we_ref, wj_ref, first_ref, valid_ref,
                        a_ref, w_ref, b_ref, o_ref, wb_ref):
    s = pl.program_id(0)

    @pl.when(first_ref[s] == 1)
    def _():
        wb_ref[...] = w_ref[...].astype(BF16)

    @pl.when(valid_ref[s] == 1)
    def _():
        o_ref[...] = _bdot(a_ref[...], wb_ref[...]) + b_ref[...]

    @pl.when(valid_ref[s] == 0)
    def _():
        o_ref[...] = jnp.zeros(o_ref.shape, o_ref.dtype)


def expert_down(layer, act, w_dn, b_dn, steps):
    tn = EXPERT_TN
    nj = D_MODEL // tn
    n_steps = N_BLK * nj
    return pl.pallas_call(
        _expert_down_kernel,
        grid_spec=pltpu.PrefetchScalarGridSpec(
            num_scalar_prefetch=6,
            grid=(n_steps,),
            in_specs=[
                pl.BlockSpec((EXPERT_ROWS, D_EXPERT), lambda s, xb, oj, we, wj, fi, va: (xb[s], 0)),
                pl.BlockSpec((None, None, D_EXPERT, tn),
                             lambda s, xb, oj, we, wj, fi, va: (layer, we[s], 0, wj[s])),
                pl.BlockSpec((None, None, 1, tn),
                             lambda s, xb, oj, we, wj, fi, va: (layer, we[s], 0, wj[s])),
            ],
            out_specs=pl.BlockSpec((EXPERT_ROWS, tn), lambda s, xb, oj, we, wj, fi, va: (xb[s], oj[s])),
            scratch_shapes=[pltpu.VMEM((D_EXPERT, tn), BF16)],
        ),
        out_shape=jax.ShapeDtypeStruct((N_SLOTS, D_MODEL), F32),
        compiler_params=_cparams("arbitrary"),
        name="expert_down",
    )(steps["x_blk"], steps["o_j"], steps["w_e"], steps["w_j"], steps["first"], steps["valid"],
      act, w_dn, b_dn)


def _combine_kernel(final, idx_ref, ys_ref, gt_ref, x_ref, g2_ref, nf_ref, o_ref, buf_ref, sem):
    tb = x_ref.shape[0]

    def copy(k, r, slot):
        return pltpu.make_async_copy(ys_ref.at[pl.ds(slot, 1)], buf_ref.at[k, pl.ds(r, 1)], sem)

    for k in range(TOP_K):
        def issue(r, c, k=k):
            copy(k, r, idx_ref[0, k * tb + r]).start()
            return c
        lax.fori_loop(0, tb, issue, 0)
    for k in range(TOP_K):
        def drain(r, c, k=k):
            copy(k, r, 0).wait()
            return c
        lax.fori_loop(0, tb, drain, 0)

    gt = gt_ref[...]
    acc = gt[:, 0:1] * buf_ref[0]
    for k in range(1, TOP_K):
        acc += gt[:, k:k + 1] * buf_ref[k]
    x2 = x_ref[...] + g2_ref[...] * acc
    if final:
        x2 = _rms(x2, nf_ref[...])
    o_ref[...] = x2


def combine(ys, dest, gates, x1, g2, norm_f, final):
    tb = COMBINE_ROWS
    t = x1.shape[0]
    nb = t // tb
    idx = dest.reshape(nb, tb, TOP_K).transpose(0, 2, 1).reshape(nb, 1, TOP_K * tb)
    return pl.pallas_call(
        functools.partial(_combine_kernel, final),
        grid=(nb,),
        in_specs=[
            pl.BlockSpec((None, 1, TOP_K * tb), lambda i: (i, 0, 0), memory_space=pltpu.SMEM),
            pl.BlockSpec(memory_space=pl.ANY),
            pl.BlockSpec((tb, 128), lambda i: (i, 0)),
            pl.BlockSpec((tb, D_MODEL), lambda i: (i, 0)),
            pl.BlockSpec((None, 1, D_MODEL), lambda i: (_mod_row(i, tb), 0, 0)),
            pl.BlockSpec((1, D_MODEL), lambda i: (0, 0)),
        ],
        out_specs=pl.BlockSpec((tb, D_MODEL), lambda i: (i, 0)),
        out_shape=jax.ShapeDtypeStruct((t, D_MODEL), F32),
        scratch_shapes=[pltpu.VMEM((TOP_K, tb, D_MODEL), F32), pltpu.SemaphoreType.DMA(())],
        compiler_params=_cparams("arbitrary"),
        name="combine",
    )(idx, ys, gates, x1, g2, norm_f)


def _rope_tables():
    rows = DEC_SEQ // GRID_W
    row = jnp.repeat(jnp.arange(rows, dtype=F32), GRID_W)
    col = jnp.tile(jnp.arange(GRID_W, dtype=F32), rows)
    half = QK_ROPE // 2
    inv = 1.0 / (ROPE_THETA ** (jnp.arange(0, half, 2, dtype=F32) / half))
    ar = row[:, None] * inv
    ac = col[:, None] * inv
    ang = jnp.concatenate([ar, ar, ac, ac], axis=-1)
    cos = jnp.concatenate([jnp.ones((SEQ, QK_ROPE), F32), jnp.cos(ang)], axis=0)
    sin = jnp.concatenate([jnp.zeros((SEQ, QK_ROPE), F32), jnp.sin(ang)], axis=0)
    scale = (QK_NOPE + QK_ROPE) ** -0.5
    tab_q = jnp.concatenate([jnp.ones((TAB_ROWS, QK_NOPE), F32), cos, sin], axis=-1) * scale
    tab_k = jnp.concatenate([cos, sin], axis=-1)
    return tab_q, tab_k


def _rotate_cols(w):
    r1, r2, c1, c2 = jnp.split(w, 4, axis=-1)
    return jnp.concatenate([-r2, r1, -c2, c1], axis=-1)


def _layer_weights(w_in, w_uq, w_uk, w_uv, w_out, s5_w_glu, w_router, b_router):
    hy, q, kv, kr, s5 = jnp.split(w_in, [HY_PROJ, HY_PROJ + Q_RANK, HY_PROJ + Q_RANK + KV_RANK,
                                         HY_PROJ + Q_RANK + KV_RANK + QK_ROPE], axis=-1)
    w_in_aug = jnp.concatenate([hy, q, kv, kr, _rotate_cols(kr), s5], axis=-1).astype(BF16)
    wq = w_uq.reshape(Q_RANK, MLA_HEADS, QK_NOPE + QK_ROPE)
    wq_rope = wq[..., QK_NOPE:]
    w_q = jnp.concatenate([wq[..., :QK_NOPE], wq_rope, _rotate_cols(wq_rope)], axis=-1)
    w_q = w_q.reshape(Q_RANK, MLA_HEADS * QH).astype(BF16)
    wk = w_uk.reshape(KV_RANK, MLA_HEADS, QK_NOPE)
    wk = jnp.concatenate([wk, jnp.zeros((KV_RANK, MLA_HEADS, 2 * QK_ROPE), F32)], axis=-1)
    eye = jnp.eye(QK_ROPE, dtype=F32)
    ek = jnp.concatenate([jnp.zeros((QK_ROPE, QK_NOPE), F32), eye, eye], axis=-1)
    ek = jnp.broadcast_to(ek[:, None, :], (QK_ROPE, MLA_HEADS, QH))
    wk = jnp.concatenate([wk, ek, jnp.zeros((QK_ROPE, MLA_HEADS, QH), F32)], axis=0)
    wv = jnp.concatenate([w_uv, jnp.zeros((2 * QK_ROPE, MLA_W), F32)], axis=0)
    w_kv = jnp.concatenate([wk.reshape(KVR_W, MLA_HEADS * QH), wv], axis=-1).astype(BF16)
    w_r = jnp.concatenate([w_router, jnp.zeros((D_MODEL, 128 - N_EXPERTS), F32)], axis=-1)
    b_r = jnp.concatenate([b_router, jnp.zeros((128 - N_EXPERTS,), F32)])[None, :]
    return w_in_aug, w_q, w_kv, w_out.astype(BF16), s5_w_glu.astype(BF16), w_r, b_r


def _s5_block_diag(lbr, lbi, bbr, bbi, c_re, c_im):
    eye = jnp.eye(S5_GROUPS, dtype=F32)

    def bd_in(b):
        return jnp.einsum("dgpn,gh->dgnhp", b, eye).reshape(2, S5_W, S5_NS).astype(BF16)

    def bd_out(c):
        return jnp.einsum("dgnp,gh->dgphn", c, eye).reshape(2, S5_NS, S5_W).astype(BF16)

    lam = jnp.concatenate([lbr.reshape(2, 1, S5_NS), lbi.reshape(2, 1, S5_NS)], axis=-1)
    lam8 = jnp.broadcast_to(lam, (2, S5_SUB, 2 * S5_NS))
    return bd_in(bbr), bd_in(bbi), bd_out(c_re), bd_out(c_im), lam8


def _time_major(z, nb, n, bp):
    u = z.reshape(nb, n, S5_W).transpose(1, 0, 2)
    if bp > nb:
        u = jnp.concatenate([u, jnp.zeros((n, bp - nb, S5_W), u.dtype)], axis=1)
    return u


def kernel(x_prompt, x_sample, cache_ckv, cache_krope, state_ssm, c, c_ctx, w_ada, b_ada, norm1, norm2, w_in, w_out, hy_conv_w, hy_conv_b, hy_fw1, hy_fb1, hy_fw2, hy_fb2, hy_freq, hy_fw3, hy_decay, hy_skip, q_norm, kv_norm, w_uq, w_uk, w_uv, s5_lam_re, s5_lam_im, s5_log_dt, s5_b_re, s5_b_im, s5_c_re, s5_c_im, s5_d, s5_w_glu, s5_b_glu, w_router, b_router, w_gate_up, b_gate_up, w_down, b_down, norm_f):
    x = jnp.concatenate([x_prompt.reshape(T_P, D_MODEL), x_sample.reshape(T_S, D_MODEL)], axis=0)

    cond8 = jnp.concatenate([c_ctx[None, :], c, jnp.zeros((N_MOD - 1 - DEC_BATCH, D_MODEL), F32)], axis=0)
    mod = ada_modulation(cond8, w_ada, b_ada).reshape(DEPTH, N_MOD, 6, 1, D_MODEL)

    tab_q, tab_k = _rope_tables()
    dft = {}
    for n in (SEQ, DEC_SEQ):
        fwd, inv = _dft_tables(n)
        f_hi, f_lo = _split_bf16(fwd)
        dft[n] = (f_hi, f_lo, jnp.asarray(inv, F32).astype(BF16))
    fw1p = jnp.concatenate([hy_fw1, jnp.zeros((DEPTH, FILT_HID - POS_EMB, FILT_HID), F32)], axis=1)
    spectra = {n: hyena_spectra(n, fw1p, hy_fb1[:, None, :], hy_fw2, hy_fb2[:, None, :], hy_freq,
                                hy_fw3, hy_decay[:, None, :], dft[n][0], dft[n][1])
               for n in (SEQ, DEC_SEQ)}

    lbr, lbi, bbr, bbi = s5_params(s5_lam_re, s5_lam_im, s5_log_dt, s5_b_re, s5_b_im)
    b_gu4 = b_gate_up.reshape(DEPTH, N_EXPERTS, 1, 2 * D_EXPERT)
    b_dn4 = b_down.reshape(DEPTH, N_EXPERTS, 1, D_MODEL)
    bp_s = S5_SUB

    ckv_out, kr_out, ssm_out = [], [], []
    for l in range(DEPTH):
        w_in_aug, w_q, w_kv, w_out_b, w_glu_b, w_r, b_r = _layer_weights(
            w_in[l], w_uq[l], w_uk[l], w_uv[l], w_out[l], s5_w_glu[l], w_router[l], b_router[l])
        m = lambda k: mod[l, :, k]
        z_hy, z_q, z_kvr, z_s5 = in_proj(x, norm1[l][None, :], m(1), m(0), w_in_aug)

        cw, cb, sk = hy_conv_w[l], hy_conv_b[l][None, :], hy_skip[l]
        y_hy = hyena(z_hy, cw, cb, dft[SEQ][0], dft[SEQ][2], spectra[SEQ][l], sk,
                     n=SEQ, nbatch=BATCH, row0=0, nseq=8, t_out=T_ALL,
                     prev=jnp.zeros((T_ALL, HY_W), BF16))
        y_hy = hyena(z_hy, cw, cb, dft[DEC_SEQ][0], dft[DEC_SEQ][2], spectra[DEC_SEQ][l], sk,
                     n=DEC_SEQ, nbatch=DEC_BATCH, row0=T_P, nseq=1, t_out=T_ALL, prev=y_hy)

        q = q_proj(z_q, q_norm[l][None, :], w_q, tab_q)
        a = kv_prep(z_kvr, kv_norm[l][None, :], tab_k)
        a_p = a[:T_P]
        ckv_out.append(a_p[:, :KV_RANK].reshape(BATCH, SEQ, KV_RANK))
        kr_out.append(a_p[:, KV_RANK:KV_RANK + QK_ROPE].reshape(BATCH, SEQ, QK_ROPE))
        ctx = jnp.concatenate([cache_ckv[:, l], cache_krope[:, l],
                               jnp.zeros((DEC_BATCH, PAST_LEN, QK_ROPE), F32)], axis=-1)
        a_s = jnp.concatenate([a[T_P:].reshape(DEC_BATCH, DEC_SEQ, KVR_W), ctx], axis=1)
        lk_s = DEC_SEQ + PAST_LEN
        k_p, v_p = kv_up(a_p, w_kv)
        k_s, v_s = kv_up(a_s.reshape(DEC_BATCH * lk_s, KVR_W), w_kv)
        y_att = attention(q, k_p, v_p, nbatch=BATCH, lq=SEQ, lk=SEQ, row0=0, t_out=T_ALL,
                          prev=jnp.zeros((T_ALL, MLA_W), BF16))
        y_att = attention(q, k_s, v_s, nbatch=DEC_BATCH, lq=DEC_SEQ, lk=lk_s, row0=T_P, t_out=T_ALL,
                          prev=y_att)

        b_re, b_im, c_re, c_im, lam8 = _s5_block_diag(lbr[l], lbi[l], bbr[l], bbi[l],
                                                      s5_c_re[l], s5_c_im[l])
        u_p = _time_major(z_s5[:T_P], BATCH, SEQ, BATCH)
        u_s = _time_major(z_s5[T_P:], DEC_BATCH, DEC_SEQ, bp_s)
        st = state_ssm[:, l].reshape(DEC_BATCH, 2, 2 * S5_NS).transpose(1, 0, 2)
        x0_s = jnp.concatenate([st, jnp.zeros((2, bp_s - DEC_BATCH, 2 * S5_NS), F32)], axis=1)
        x0_p = jnp.zeros((2, BATCH, 2 * S5_NS), F32)
        y2_p, xf_p = s5_scan(u_p, b_re, b_im, c_re, c_im, lam8, x0_p)
        y2_s, _ = s5_scan(u_s, b_re, b_im, c_re, c_im, lam8, x0_s)
        ssm_out.append(xf_p.reshape(2, BATCH, 2, S5_GROUPS, S5_STATE).transpose(1, 0, 2, 3, 4))
        d_row, bg_row = s5_d[l][None, :], s5_b_glu[l][None, :]
        ys_p = s5_glu(y2_p.reshape(2, SEQ * BATCH, S5_W), u_p.reshape(SEQ * BATCH, S5_W),
                      d_row, w_glu_b, bg_row)
        ys_s = s5_glu(y2_s.reshape(2, DEC_SEQ * bp_s, S5_W), u_s.reshape(DEC_SEQ * bp_s, S5_W),
                      d_row, w_glu_b, bg_row)
        y_s5 = jnp.concatenate([
            ys_p.reshape(SEQ, BATCH, S5_W).transpose(1, 0, 2).reshape(T_P, S5_W),
            ys_s.reshape(DEC_SEQ, bp_s, S5_W)[:, :DEC_BATCH].transpose(1, 0, 2).reshape(T_S, S5_W)], axis=0)

        x1, h2p, top_i, gates = out_proj(y_hy, y_att, y_s5, w_out_b, x, m(2), norm2[l][None, :],
                                         m(4), m(3), w_r, b_r)

        slot_tok, dest, steps = route_tables(top_i[:, :TOP_K])
        xs = gather_rows(h2p, slot_tok, EXPERT_ROWS)
        act = expert_up(l, xs, w_gate_up, b_gu4, steps)
        ys = expert_down(l, act, w_down, b_dn4, steps)
        x = combine(ys, dest, gates, x1, m(5), norm_f[None, :], final=(l == DEPTH - 1))

    y_prompt = x[:T_P].reshape(BATCH, SEQ, D_MODEL)
    y_sample = x[T_P:].reshape(DEC_BATCH, DEC_SEQ, D_MODEL)
    return (y_prompt, y_sample, jnp.stack(ckv_out, axis=1), jnp.stack(kr_out, axis=1),
            jnp.stack(ssm_out, axis=1))
```

```python
import functools
import math

import numpy as np
import jax
import jax.numpy as jnp
from jax import lax
from jax.experimental import pallas as pl
from jax.experimental.pallas import tpu as pltpu

F32 = jnp.float32
BF16 = jnp.bfloat16
I32 = jnp.int32
U32 = jnp.uint32
HIGHEST = lax.Precision.HIGHEST

D_MODEL = 2048
BATCH = 32
SEQ = 256
DEPTH = 4
DEC_BATCH = 4
DEC_SEQ = 1024
PAST_LEN = 512
GRID_W = 64
EPS = 1e-6
HY_W = 512
HY_PROJ = 3 * HY_W
POS_BANDS = 16
POS_EMB = 2 * POS_BANDS + 1
FILT_HID = 64
MLA_HEADS = 8
Q_RANK = 512
KV_RANK = 256
QK_NOPE = 128
QK_ROPE = 64
V_DIM = 128
MLA_W = MLA_HEADS * V_DIM
ROPE_THETA = 10000.0
S5_W = 512
S5_GROUP_CH = 16
S5_GROUPS = 32
S5_STATE = 64
S5_NS = S5_GROUPS * S5_STATE
N_EXPERTS = 32
TOP_K = 4
D_EXPERT = D_MODEL
SWIGLU_LIMIT = 7.0
SWIGLU_ALPHA = 1.702

T_P = BATCH * SEQ
T_S = DEC_BATCH * DEC_SEQ
T_ALL = T_P + T_S
N_MOD = 8
QH = 256
KVR_W = KV_RANK + 2 * QK_ROPE
ROW_TILE = 256
TAB_ROWS = SEQ + DEC_SEQ

VMEM_LIMIT_BYTES = 56 * 1024 * 1024

EXPERT_ROWS = 512
N_ASSIGN = T_ALL * TOP_K
ROUTE_ROWS = 256
N_RBLK = T_ALL // ROUTE_ROWS
SEG_ALIGN = 8
SEG_MAX = ROUTE_ROWS
SLOT_ROWS = ROUTE_ROWS * TOP_K + N_EXPERTS * SEG_ALIGN
N_BLK = (N_ASSIGN + N_RBLK * N_EXPERTS * (SEG_ALIGN - 1)) // EXPERT_ROWS + 1 + N_EXPERTS
N_SLOTS = N_BLK * EXPERT_ROWS
EXPERT_TN = 512
EXPERT_CHUNK = 128
MM_ROWS = 512


def _cparams(*sem):
    return pltpu.CompilerParams(dimension_semantics=sem, vmem_limit_bytes=VMEM_LIMIT_BYTES)


def _mod_row(i, tm):
    npb = T_P // tm
    sb = DEC_SEQ // tm
    return jnp.where(i < npb, 0, 1 + (i - npb) // sb)


def _tab_blk(i, tm):
    npb = T_P // tm
    pb = SEQ // tm
    sb = DEC_SEQ // tm
    return jnp.where(i < npb, i % pb, pb + (i - npb) % sb)


def _rms(x, g):
    return x * lax.rsqrt(jnp.mean(x * x, axis=-1, keepdims=True) + EPS) * g


def _bdot(a, b):
    return jnp.dot(a, b, preferred_element_type=F32)


def _ada_kernel(c_ref, w_ref, b_ref, o_ref):
    s = jax.nn.silu(c_ref[...]).astype(BF16)
    o_ref[...] = _bdot(s, w_ref[...].astype(BF16)) + b_ref[...]


def ada_modulation(cond8, w_ada, b_ada):
    tn = 1024
    n = 6 * D_MODEL
    return pl.pallas_call(
        _ada_kernel,
        grid=(DEPTH, n // tn),
        in_specs=[
            pl.BlockSpec((N_MOD, D_MODEL), lambda l, j: (0, 0)),
            pl.BlockSpec((None, D_MODEL, tn), lambda l, j: (l, 0, j)),
            pl.BlockSpec((None, 1, tn), lambda l, j: (l, 0, j)),
        ],
        out_specs=pl.BlockSpec((None, N_MOD, tn), lambda l, j: (l, 0, j)),
        out_shape=jax.ShapeDtypeStruct((DEPTH, N_MOD, n), F32),
        compiler_params=_cparams("arbitrary", "arbitrary"),
        name="ada_modulation",
    )(cond8, w_ada, b_ada.reshape(DEPTH, 1, n))


IN_COLS = (HY_PROJ, Q_RANK, KVR_W, S5_W)
IN_W_AUG = sum(IN_COLS)


def _in_proj_kernel(x_ref, g_ref, sc_ref, sh_ref, w_ref, zhy_ref, zq_ref, zkvr_ref, zs5_ref):
    h = _rms(x_ref[...], g_ref[...]) * (1.0 + sc_ref[...]) + sh_ref[...]
    h = h.astype(BF16)
    off = 0
    for o_ref, wd in zip((zhy_ref, zq_ref, zkvr_ref, zs5_ref), IN_COLS):
        o_ref[...] = _bdot(h, w_ref[:, off:off + wd])
        off += wd


def in_proj(x, g, sc, sh, w_aug):
    tm = MM_ROWS
    t = x.shape[0]
    mod_spec = pl.BlockSpec((None, 1, D_MODEL), lambda i: (_mod_row(i, tm), 0, 0))
    return pl.pallas_call(
        _in_proj_kernel,
        grid=(t // tm,),
        in_specs=[
            pl.BlockSpec((tm, D_MODEL), lambda i: (i, 0)),
            pl.BlockSpec((1, D_MODEL), lambda i: (0, 0)),
            mod_spec, mod_spec,
            pl.BlockSpec((D_MODEL, IN_W_AUG), lambda i: (0, 0)),
        ],
        out_specs=[pl.BlockSpec((tm, wd), lambda i: (i, 0)) for wd in IN_COLS],
        out_shape=[jax.ShapeDtypeStruct((t, wd), F32) for wd in IN_COLS],
        compiler_params=_cparams("arbitrary"),
        name="in_proj",
    )(x, g, sc, sh, w_aug)


def _dft_tables(n):
    f = np.arange(n)[:, None]
    s = np.arange(n)[None, :]
    ang = (np.pi / n) * ((f * s) % (2 * n)).astype(np.float64)
    cos = np.cos(ang)
    sin = np.sin(ang)
    nyq = np.where(np.arange(n) % 2 == 0, 1.0, -1.0)
    fs = sin.copy()
    fs[0, :] = nyq
    fwd = np.concatenate([cos, fs], axis=0)
    wgt = np.full((n,), 2.0)
    wgt[0] = 1.0
    gc = (cos * wgt[:, None]).T / (2 * n)
    gs = (sin * wgt[:, None]).T / (2 * n)
    gs[:, 0] = nyq / (2 * n)
    inv = np.concatenate([gc, gs], axis=1)
    return fwd, inv


def _split_bf16(a):
    hi = jnp.asarray(a, F32).astype(BF16)
    lo = (jnp.asarray(a, F32) - hi.astype(F32)).astype(BF16)
    return hi, lo


def _hyena_spec_kernel(n, feat_ref, t_ref, fw1_ref, fb1_ref, fw2_ref, fb2_ref, freq_ref,
                       w3f0_ref, w3f1_ref, w3b0_ref, w3b1_ref,
                       dcf0_ref, dcf1_ref, dcb0_ref, dcb1_ref,
                       fhi_ref, flo_ref, o_ref):
    hdot = functools.partial(jnp.dot, precision=HIGHEST, preferred_element_type=F32)
    h = jnp.sin(freq_ref[0:1, :] * (hdot(feat_ref[...], fw1_ref[...]) + fb1_ref[...]))
    h = jnp.sin(freq_ref[1:2, :] * (hdot(h, fw2_ref[...]) + fb2_ref[...]))
    tcol = t_ref[...]
    row0 = lax.broadcasted_iota(I32, (n, 1), 0) == 0

    def filt(w3_ref, dc_ref):
        return hdot(h, w3_ref[...]) * jnp.exp(-tcol * jnp.abs(dc_ref[...]))

    def dft(fpart_hi, fpart_lo, k):
        k_hi = k.astype(BF16)
        k_lo = (k - k_hi.astype(F32)).astype(BF16)
        return _bdot(fpart_hi, k_hi) + (_bdot(fpart_hi, k_lo) + _bdot(fpart_lo, k_hi))

    for o, (wf, wb, df, db) in enumerate(((w3f0_ref, w3b0_ref, dcf0_ref, dcb0_ref),
                                          (w3f1_ref, w3b1_ref, dcf1_ref, dcb1_ref))):
        fwd = filt(wf, df)
        bwd = jnp.where(row0, 0.0, filt(wb, db))
        ks = dft(fhi_ref[...], flo_ref[...], fwd + bwd)
        kd = dft(fhi_ref[n:, :], flo_ref[n:, :], fwd - bwd)
        kr = ks[:n]
        ksp = jnp.where(row0, ks[n:n + 1], kd)
        o_ref[3 * o + 0] = kr
        o_ref[3 * o + 1] = jnp.where(row0, 0.0, ksp)
        o_ref[3 * o + 2] = jnp.where(row0, ksp, kr)


def hyena_spectra(n, fw1p, fb1, fw2, fb2, freq, fw3, decay, fwd_hi, fwd_lo):
    tc = 256
    nc = HY_W // tc
    f32 = F32
    t = jnp.linspace(0.0, 1.0, n, dtype=f32)[:, None]
    w = (2.0 * math.pi / n) * jnp.arange(n, dtype=f32)[:, None]
    bands = jnp.linspace(1e-4, POS_BANDS - 1, POS_BANDS, dtype=f32)
    feat = jnp.concatenate([t, jnp.cos(w * bands), -jnp.sin(w * bands),
                            jnp.zeros((n, FILT_HID - POS_EMB), f32)], axis=-1)

    def w3_spec(d, o):
        return pl.BlockSpec((None, FILT_HID, tc), lambda l, c: (l, 0, (d * 2 + o) * nc + c))

    def dc_spec(d, o):
        return pl.BlockSpec((None, 1, tc), lambda l, c: (l, 0, (d * 2 + o) * nc + c))

    const2 = lambda shape: pl.BlockSpec(shape, lambda l, c: (0, 0))
    lay3 = lambda a, b: pl.BlockSpec((None, a, b), lambda l, c: (l, 0, 0))
    return pl.pallas_call(
        functools.partial(_hyena_spec_kernel, n),
        grid=(DEPTH, nc),
        in_specs=[
            const2((n, FILT_HID)), const2((n, 1)),
            lay3(FILT_HID, FILT_HID), lay3(1, FILT_HID), lay3(FILT_HID, FILT_HID), lay3(1, FILT_HID),
            lay3(2, FILT_HID),
            w3_spec(0, 0), w3_spec(0, 1), w3_spec(1, 0), w3_spec(1, 1),
            dc_spec(0, 0), dc_spec(0, 1), dc_spec(1, 0), dc_spec(1, 1),
            const2((2 * n, n)), const2((2 * n, n)),
        ],
        out_specs=pl.BlockSpec((None, 6, n, tc), lambda l, c: (l, 0, 0, c)),
        out_shape=jax.ShapeDtypeStruct((DEPTH, 6, n, HY_W), F32),
        compiler_params=_cparams("arbitrary", "arbitrary"),
        name=f"hyena_spectra_{n}",
    )(feat, t, fw1p, fb1, fw2, fb2, freq, fw3, fw3, fw3, fw3, decay, decay, decay, decay,
      fwd_hi, fwd_lo)


def _hyena_kernel(n, nseq, zv_ref, z1_ref, z2_ref, wv_ref, w1_ref, w2_ref, bv_ref, b1_ref, b2_ref,
                  f_ref, g_ref, spec_ref, skip_ref, *rest):
    o_ref = rest[-1]
    row = lax.broadcasted_iota(I32, (n, 1), 0)
    first = row == 0
    last = row == n - 1

    def conv3(z, w_ref, b_ref):
        zm = jnp.where(first, 0.0, pltpu.roll(z, 1, 0))
        zp = jnp.where(last, 0.0, pltpu.roll(z, n - 1, 0))
        return zm * w_ref[0:1, :] + z * w_ref[1:2, :] + zp * w_ref[2:3, :] + b_ref[...]

    def long_conv(u, o):
        uu = _bdot(f_ref[...], u.astype(BF16))
        a = uu[:n]
        b = uu[n:]
        kr = spec_ref[3 * o + 0]
        ks = spec_ref[3 * o + 1]
        dd = spec_ref[3 * o + 2]
        yre = (kr * a - ks * b).astype(BF16)
        zz = (dd * b + ks * a).astype(BF16)
        y = _bdot(g_ref[:, :n], yre) + _bdot(g_ref[:, n:], zz)
        return y + u * skip_ref[o:o + 1, :]

    for s in range(nseq):
        rows = pl.ds(s * n, n)
        v = conv3(zv_ref[rows, :], wv_ref, bv_ref)
        x1 = conv3(z1_ref[rows, :], w1_ref, b1_ref)
        x2 = conv3(z2_ref[rows, :], w2_ref, b2_ref)
        y = x1 * long_conv(v, 0)
        y = x2 * long_conv(y, 1)
        o_ref[rows, :] = y.astype(o_ref.dtype)


def hyena(z_hy, conv_w, conv_b, f_hi, g_inv, spec, skip, *, n, nbatch, row0, nseq, t_out, prev=None):
    tc = 256
    nc = HY_W // tc
    rb = nseq * n
    blk0 = row0 // rb
    assert row0 % rb == 0 and nbatch % nseq == 0

    def zspec(part):
        return pl.BlockSpec((rb, tc), lambda c, b: (blk0 + b, part * nc + c))

    def wspec(part, rows):
        return pl.BlockSpec((rows, tc), lambda c, b: (0, part * nc + c))

    in_specs = [zspec(0), zspec(1), zspec(2), wspec(0, 3), wspec(1, 3), wspec(2, 3),
                wspec(0, 1), wspec(1, 1), wspec(2, 1),
                pl.BlockSpec((2 * n, n), lambda c, b: (0, 0)),
                pl.BlockSpec((n, 2 * n), lambda c, b: (0, 0)),
                pl.BlockSpec((6, n, tc), lambda c, b: (0, 0, c)),
                pl.BlockSpec((2, tc), lambda c, b: (0, c))]
    args = [z_hy, z_hy, z_hy, conv_w, conv_w, conv_w, conv_b, conv_b, conv_b, f_hi, g_inv, spec, skip]
    aliases = {}
    if prev is not None:
        in_specs.append(pl.BlockSpec(memory_space=pl.ANY))
        args.append(prev)
        aliases = {len(args) - 1: 0}
    return pl.pallas_call(
        functools.partial(_hyena_kernel, n, nseq),
        grid=(nc, nbatch // nseq),
        in_specs=in_specs,
        out_specs=pl.BlockSpec((rb, tc), lambda c, b: (blk0 + b, c)),
        out_shape=jax.ShapeDtypeStruct((t_out, HY_W), BF16),
        input_output_aliases=aliases,
        compiler_params=_cparams("arbitrary", "arbitrary"),
        name=f"hyena_{n}",
    )(*args)


def _q_proj_kernel(z_ref, g_ref, w_ref, tab_ref, o_ref):
    h = _rms(z_ref[...], g_ref[...]).astype(BF16)
    tab = tab_ref[...]
    for hd in range(MLA_HEADS):
        cols = slice(hd * QH, (hd + 1) * QH)
        o_ref[:, cols] = (_bdot(h, w_ref[:, cols]) * tab).astype(BF16)


def q_proj(z_q, g, w_q, tab_q):
    tm = ROW_TILE
    t = z_q.shape[0]
    return pl.pallas_call(
        _q_proj_kernel,
        grid=(t // tm,),
        in_specs=[
            pl.BlockSpec((tm, Q_RANK), lambda i: (i, 0)),
            pl.BlockSpec((1, Q_RANK), lambda i: (0, 0)),
            pl.BlockSpec((Q_RANK, MLA_HEADS * QH), lambda i: (0, 0)),
            pl.BlockSpec((tm, QH), lambda i: (_tab_blk(i, tm), 0)),
        ],
        out_specs=pl.BlockSpec((tm, MLA_HEADS * QH), lambda i: (i, 0)),
        out_shape=jax.ShapeDtypeStruct((t, MLA_HEADS * QH), BF16),
        compiler_params=_cparams("arbitrary"),
        name="q_proj",
    )(z_q, g, w_q, tab_q)


def _kv_prep_kernel(z_ref, g_ref, tab_ref, o_ref):
    o_ref[:, :KV_RANK] = _rms(z_ref[:, :KV_RANK], g_ref[...])
    t = z_ref[:, KV_RANK:] * tab_ref[...]
    o_ref[:, KV_RANK:] = t + pltpu.roll(t, QK_ROPE, 1)


def kv_prep(z_kvr, g, tab_k):
    tm = ROW_TILE
    t = z_kvr.shape[0]
    return pl.pallas_call(
        _kv_prep_kernel,
        grid=(t // tm,),
        in_specs=[
            pl.BlockSpec((tm, KVR_W), lambda i: (i, 0)),
            pl.BlockSpec((1, KV_RANK), lambda i: (0, 0)),
            pl.BlockSpec((tm, 2 * QK_ROPE), lambda i: (_tab_blk(i, tm), 0)),
        ],
        out_specs=pl.BlockSpec((tm, KVR_W), lambda i: (i, 0)),
        out_shape=jax.ShapeDtypeStruct((t, KVR_W), F32),
        compiler_params=_cparams("arbitrary"),
        name="kv_prep",
    )(z_kvr, g, tab_k)


def _kv_up_kernel(a_ref, w_ref, k_ref, v_ref):
    a = a_ref[...].astype(BF16)
    nk = MLA_HEADS * QH
    k_ref[...] = _bdot(a, w_ref[:, :nk]).astype(BF16)
    v_ref[...] = _bdot(a, w_ref[:, nk:]).astype(BF16)


def kv_up(a, w_kv):
    tm = 512
    t = a.shape[0]
    nk = MLA_HEADS * QH
    return pl.pallas_call(
        _kv_up_kernel,
        grid=(t // tm,),
        in_specs=[
            pl.BlockSpec((tm, KVR_W), lambda i: (i, 0)),
            pl.BlockSpec((KVR_W, nk + MLA_W), lambda i: (0, 0)),
        ],
        out_specs=[pl.BlockSpec((tm, nk), lambda i: (i, 0)),
                   pl.BlockSpec((tm, MLA_W), lambda i: (i, 0))],
        out_shape=[jax.ShapeDtypeStruct((t, nk), BF16), jax.ShapeDtypeStruct((t, MLA_W), BF16)],
        compiler_params=_cparams("arbitrary"),
        name="kv_up",
    )(a, w_kv)


def _attn_kernel(q_ref, k_ref, v_ref, *rest):
    o_ref = rest[-1]
    for hd in range(MLA_HEADS):
        q = q_ref[:, hd * QH:(hd + 1) * QH]
        k = k_ref[:, hd * QH:(hd + 1) * QH]
        s = lax.dot_general(q, k, (((1,), (1,)), ((), ())), preferred_element_type=F32)
        m = jnp.max(s, axis=-1, keepdims=True)
        p = jnp.exp(s - m)
        l = jnp.sum(p, axis=-1, keepdims=True)
        o = _bdot(p.astype(BF16), v_ref[:, hd * V_DIM:(hd + 1) * V_DIM])
        o_ref[:, hd * V_DIM:(hd + 1) * V_DIM] = (o / l).astype(o_ref.dtype)


def attention(q, k, v, *, nbatch, lq, lk, row0, t_out, prev=None):
    tq = ROW_TILE
    nq = lq // tq
    blk0 = row0 // tq
    in_specs = [
        pl.BlockSpec((tq, MLA_HEADS * QH), lambda b, i: (blk0 + b * nq + i, 0)),
        pl.BlockSpec((lk, MLA_HEADS * QH), lambda b, i: (b, 0)),
        pl.BlockSpec((lk, MLA_W), lambda b, i: (b, 0)),
    ]
    args = [q, k, v]
    aliases = {}
    if prev is not None:
        in_specs.append(pl.BlockSpec(memory_space=pl.ANY))
        args.append(prev)
        aliases = {3: 0}
    return pl.pallas_call(
        _attn_kernel,
        grid=(nbatch, nq),
        in_specs=in_specs,
        out_specs=pl.BlockSpec((tq, MLA_W), lambda b, i: (blk0 + b * nq + i, 0)),
        out_shape=jax.ShapeDtypeStruct((t_out, MLA_W), BF16),
        input_output_aliases=aliases,
        compiler_params=_cparams("arbitrary", "arbitrary"),
        name=f"attention_{lk}",
    )(*args)


def _s5_param_kernel(lr_ref, li_ref, dt_ref, lrx_ref, lix_ref, dtx_ref, bre_ref, bim_ref,
                     lbr_ref, lbi_ref, bbr_ref, bbi_ref):
    def lam_bar(lr, li, ldt):
        lr = jnp.minimum(lr, -1e-4)
        dt = jnp.exp(ldt)
        e = jnp.exp(lr * dt)
        return lr, e * jnp.cos(li * dt), e * jnp.sin(li * dt)

    _, lbr, lbi = lam_bar(lr_ref[...], li_ref[...], dt_ref[...])
    lbr_ref[...] = lbr
    lbi_ref[...] = lbi
    lr, xr, xi = lam_bar(lrx_ref[...], lix_ref[...], dtx_ref[...])
    li = lix_ref[...]
    nr = xr - 1.0
    den = lr * lr + li * li
    cr = (nr * lr + xi * li) / den
    ci = (xi * lr - nr * li) / den
    bbr_ref[...] = cr * bre_ref[...] - ci * bim_ref[...]
    bbi_ref[...] = cr * bim_ref[...] + ci * bre_ref[...]


def s5_params(lam_re, lam_im, log_dt, b_re, b_im):
    r = DEPTH * 2 * S5_GROUPS
    ldt = jnp.broadcast_to(log_dt[..., None], lam_re.shape)
    small = [a.reshape(r, S5_STATE) for a in (lam_re, lam_im, ldt)]
    wide = [jnp.repeat(a, S5_GROUP_CH, axis=-1) for a in small]
    bs = [a.reshape(r, S5_STATE * S5_GROUP_CH) for a in (b_re, b_im)]
    outs = pl.pallas_call(
        _s5_param_kernel,
        out_shape=[jax.ShapeDtypeStruct((r, S5_STATE), F32)] * 2
        + [jax.ShapeDtypeStruct((r, S5_STATE * S5_GROUP_CH), F32)] * 2,
        name="s5_params",
    )(*small, *wide, *bs)
    lbr, lbi, bbr, bbi = outs
    shp = (DEPTH, 2, S5_GROUPS, S5_STATE)
    return (lbr.reshape(shp), lbi.reshape(shp),
            bbr.reshape(shp + (S5_GROUP_CH,)), bbi.reshape(shp + (S5_GROUP_CH,)))


S5_SUB = 8
S5_LANES = 512
S5_CH = S5_LANES // S5_STATE * S5_GROUP_CH


def _s5_scan_kernel(tq, u_ref, bre_ref, bim_ref, cre_ref, cim_ref, lam_ref, x0_ref,
                    y_ref, xf_ref, s_ref, x_ref):
    d = pl.program_id(0)
    j = pl.program_id(2)
    rows = tq * S5_SUB

    @pl.when(j == 0)
    def _():
        x_ref[...] = x0_ref[...]

    u = u_ref[...].reshape(rows, S5_W).astype(BF16)
    for c in range(S5_NS // S5_LANES):
        ch = slice(c * S5_CH, (c + 1) * S5_CH)
        st = slice(c * S5_LANES, (c + 1) * S5_LANES)
        s_ref[:, st] = _bdot(u[:, ch], bre_ref[ch, st])
        s_ref[:, S5_NS + c * S5_LANES:S5_NS + (c + 1) * S5_LANES] = _bdot(u[:, ch], bim_ref[ch, st])

    for c in range(S5_NS // S5_LANES):
        re = pl.ds(c * S5_LANES, S5_LANES)
        im = pl.ds(S5_NS + c * S5_LANES, S5_LANES)
        lr = lam_ref[:, re]
        li = lam_ref[:, im]

        def step(i, carry):
            xr, xi = carry
            t = jnp.where(d == 0, i, tq - 1 - i)
            r = pl.ds(pl.multiple_of(t * S5_SUB, S5_SUB), S5_SUB)
            nr = lr * xr - li * xi + s_ref[r, re]
            ni = lr * xi + li * xr + s_ref[r, im]
            s_ref[r, re] = nr
            s_ref[r, im] = ni
            return nr, ni

        xr, xi = lax.fori_loop(0, tq, step, (x_ref[:, re], x_ref[:, im]))
        x_ref[:, re] = xr
        x_ref[:, im] = xi

    for c in range(S5_NS // S5_LANES):
        ch = slice(c * S5_CH, (c + 1) * S5_CH)
        st = slice(c * S5_LANES, (c + 1) * S5_LANES)
        y = (_bdot(s_ref[:, st].astype(BF16), cre_ref[st, ch])
             - _bdot(s_ref[:, S5_NS + c * S5_LANES:S5_NS + (c + 1) * S5_LANES].astype(BF16), cim_ref[st, ch]))
        y_ref[:, :, ch] = y.reshape(tq, S5_SUB, S5_CH)

    @pl.when(j == pl.num_programs(2) - 1)
    def _():
        xf_ref[...] = x_ref[...]


def s5_scan(u_tm, b_re, b_im, c_re, c_im, lam8, x0):
    n, bp, _ = u_tm.shape
    tq = 64
    nch = n // tq

    def tci(d, j):
        return d * (nch - 1) + (1 - 2 * d) * j

    dir3 = lambda a, b: pl.BlockSpec((None, a, b), lambda d, g, j: (d, 0, 0))
    return pl.pallas_call(
        functools.partial(_s5_scan_kernel, tq),
        grid=(2, bp // S5_SUB, nch),
        in_specs=[
            pl.BlockSpec((tq, S5_SUB, S5_W), lambda d, g, j: (tci(d, j), g, 0)),
            dir3(S5_W, S5_NS), dir3(S5_W, S5_NS), dir3(S5_NS, S5_W), dir3(S5_NS, S5_W),
            dir3(S5_SUB, 2 * S5_NS),
            pl.BlockSpec((None, S5_SUB, 2 * S5_NS), lambda d, g, j: (d, g, 0)),
        ],
        out_specs=[
            pl.BlockSpec((None, tq, S5_SUB, S5_W), lambda d, g, j: (d, tci(d, j), g, 0)),
            pl.BlockSpec((None, S5_SUB, 2 * S5_NS), lambda d, g, j: (d, g, 0)),
        ],
        out_shape=[jax.ShapeDtypeStruct((2, n, bp, S5_W), F32),
                   jax.ShapeDtypeStruct((2, bp, 2 * S5_NS), F32)],
        scratch_shapes=[pltpu.VMEM((tq * S5_SUB, 2 * S5_NS), F32),
                        pltpu.VMEM((S5_SUB, 2 * S5_NS), F32)],
        compiler_params=_cparams("arbitrary", "arbitrary", "arbitrary"),
        name=f"s5_scan_{n}",
    )(u_tm, b_re, b_im, c_re, c_im, lam8, x0)


def _s5_glu_kernel(y_ref, u_ref, d_ref, w_ref, b_ref, o_ref):
    y = y_ref[0] + y_ref[1] + u_ref[...] * d_ref[...]
    zg = jax.nn.gelu(y)
    gate = jax.nn.sigmoid(_bdot(zg.astype(BF16), w_ref[...]) + b_ref[...])
    o_ref[...] = (zg * gate).astype(o_ref.dtype)


def s5_glu(y2, u, d, w_glu, b_glu):
    tm = 512
    r = u.shape[0]
    return pl.pallas_call(
        _s5_glu_kernel,
        grid=(r // tm,),
        in_specs=[
            pl.BlockSpec((2, tm, S5_W), lambda i: (0, i, 0)),
            pl.BlockSpec((tm, S5_W), lambda i: (i, 0)),
            pl.BlockSpec((1, S5_W), lambda i: (0, 0)),
            pl.BlockSpec((S5_W, S5_W), lambda i: (0, 0)),
            pl.BlockSpec((1, S5_W), lambda i: (0, 0)),
        ],
        out_specs=pl.BlockSpec((tm, S5_W), lambda i: (i, 0)),
        out_shape=jax.ShapeDtypeStruct((r, S5_W), BF16),
        compiler_params=_cparams("arbitrary"),
        name="s5_glu",
    )(y2, u, d, w_glu, b_glu)


def _out_proj_kernel(yh_ref, ya_ref, ys_ref, w_ref, x_ref, g1_ref, n2_ref, sc_ref, sh_ref,
                     wrh_ref, wrl_ref, br_ref, ltri_ref, ustr_ref,
                     x1_ref, h2_ref, rt_ref, gt_ref, cnt_ref):
    acc = _bdot(yh_ref[...], w_ref[:HY_W, :])
    acc += _bdot(ya_ref[...], w_ref[HY_W:HY_W + MLA_W, :])
    acc += _bdot(ys_ref[...], w_ref[HY_W + MLA_W:, :])
    x1 = x_ref[...] + g1_ref[...] * acc
    x1_ref[...] = x1
    h2 = _rms(x1, n2_ref[...]) * (1.0 + sc_ref[...]) + sh_ref[...]
    h_hi = h2.astype(BF16)
    h2_ref[...] = h_hi

    h_lo = (h2 - h_hi.astype(F32)).astype(BF16)
    logits_all = (_bdot(h_hi, wrh_ref[...]) + (_bdot(h_hi, wrl_ref[...]) + _bdot(h_lo, wrh_ref[...]))
                  + br_ref[...])
    neg = jnp.float32(-jnp.inf)
    nsub = x_ref.shape[0] // ROUTE_ROWS
    for sb in range(nsub):
        logits = logits_all[sb * ROUTE_ROWS:(sb + 1) * ROUTE_ROWS]
        lane = lax.broadcasted_iota(I32, logits.shape, 1)
        logits = jnp.where(lane < N_EXPERTS, logits, neg)
        idxs, exps = [], []
        v0 = None
        for k in range(TOP_K):
            m = jnp.max(logits, axis=-1, keepdims=True)
            idx = jnp.min(jnp.where(logits == m, lane, 128), axis=-1, keepdims=True)
            if k == 0:
                v0 = m
            idxs.append(idx)
            exps.append(jnp.exp(m - v0))
            logits = jnp.where(lane == idx, neg, logits)
        ohs = jnp.zeros(logits.shape, F32)
        for idx in idxs:
            ohs += (lane == idx).astype(F32)
        before = _bdot(ltri_ref[...], ohs.astype(BF16))
        cnt = jnp.sum(ohs, axis=0, keepdims=True)
        cnt8 = jnp.broadcast_to(cnt, (8, 128))
        seg = jnp.floor((cnt8 + (SEG_ALIGN - 1)) * (1.0 / SEG_ALIGN)) * SEG_ALIGN
        lower = _bdot(seg.astype(BF16), ustr_ref[...])[0:1]
        base = before + lower
        rt = jnp.zeros(logits.shape, I32)
        gt = jnp.zeros(logits.shape, F32)
        den = exps[0] + exps[1] + exps[2] + exps[3]
        for k in range(TOP_K):
            pos = jnp.sum(jnp.where(lane == idxs[k], base, 0.0), axis=-1, keepdims=True)
            rt = jnp.where(lane == k, idxs[k], rt)
            rt = jnp.where(lane == TOP_K + k, pos.astype(I32), rt)
            gt = jnp.where(lane == k, exps[k] / den, gt)
        rows = slice(sb * ROUTE_ROWS, (sb + 1) * ROUTE_ROWS)
        rt_ref[rows, :] = rt
        gt_ref[rows, :] = gt
        cnt_ref[sb] = cnt8.astype(I32)


def out_proj(y_hy, y_att, y_s5, w_out, x, g1, n2, sc2, sh2, w_r_hi, w_r_lo, b_r):
    tm = MM_ROWS
    t = x.shape[0]
    nsub = tm // ROUTE_ROWS
    row = lambda wd: pl.BlockSpec((tm, wd), lambda i: (i, 0))
    const = lambda a, b: pl.BlockSpec((a, b), lambda i: (0, 0))
    mod_spec = pl.BlockSpec((None, 1, D_MODEL), lambda i: (_mod_row(i, tm), 0, 0))
    r = np.arange(ROUTE_ROWS)
    ltri = jnp.asarray(r[None, :] < r[:, None], BF16)
    e = np.arange(128)
    ustr = jnp.asarray(e[:, None] < e[None, :], BF16)
    return pl.pallas_call(
        _out_proj_kernel,
        grid=(t // tm,),
        in_specs=[row(HY_W), row(MLA_W), row(S5_W), const(D_MODEL, D_MODEL), row(D_MODEL),
                  mod_spec, const(1, D_MODEL), mod_spec, mod_spec,
                  const(D_MODEL, 128), const(D_MODEL, 128), const(1, 128),
                  const(ROUTE_ROWS, ROUTE_ROWS), const(128, 128)],
        out_specs=[row(D_MODEL), row(D_MODEL), row(128), row(128),
                   pl.BlockSpec((nsub, 8, 128), lambda i: (i, 0, 0))],
        out_shape=[jax.ShapeDtypeStruct((t, D_MODEL), F32),
                   jax.ShapeDtypeStruct((t, D_MODEL), BF16),
                   jax.ShapeDtypeStruct((t, 128), I32),
                   jax.ShapeDtypeStruct((t, 128), F32),
                   jax.ShapeDtypeStruct((t // ROUTE_ROWS, 8, 128), I32)],
        compiler_params=_cparams("arbitrary"),
        name="out_proj",
    )(y_hy, y_att, y_s5, w_out, x, g1, n2, sc2, sh2, w_r_hi, w_r_lo, b_r, ltri, ustr)


def route_tables(cnt_arr):
    rb = EXPERT_ROWS
    cnt = cnt_arr[:, 0, :N_EXPERTS]
    cnt = (cnt + SEG_ALIGN - 1) // SEG_ALIGN * SEG_ALIGN
    counts = jnp.sum(cnt, axis=0)
    blk_cnt = (counts + rb - 1) // rb
    blk_end = jnp.cumsum(blk_cnt)
    blk_start = blk_end - blk_cnt
    total_blk = blk_end[-1]
    seg_start = blk_start[None, :] * rb + jnp.cumsum(cnt, axis=0) - cnt
    tail = jnp.stack([total_blk * rb, (N_BLK - total_blk) * (rb // SEG_MAX)])
    pad = jnp.concatenate([blk_start * rb + counts, blk_cnt * rb - counts, tail])
    seg = dict(start=seg_start.reshape(-1).astype(I32), cnt=cnt.reshape(-1).astype(I32), pad=pad.astype(I32))

    nj = D_MODEL // EXPERT_TN
    s = jnp.arange(N_BLK * nj, dtype=I32)
    blk0 = s // nj
    valid = blk0 < total_blk

    def expert_of(blk):
        return jnp.minimum(jnp.sum((blk_end[None, :] <= blk[:, None]).astype(I32), axis=1), N_EXPERTS - 1)

    last_e = expert_of((total_blk - 1)[None])[0]
    e = expert_of(blk0)
    c_e = jnp.maximum(blk_cnt[e], 1)
    local = s - blk_start[e] * nj
    j = local // c_e
    r = local % c_e
    steps = dict(
        x_blk=jnp.where(valid, blk_start[e] + r, 0),
        o_blk=jnp.where(valid, blk_start[e] + r, blk0),
        o_j=jnp.where(valid, j, s % nj),
        w_e=jnp.where(valid, e, last_e),
        w_j=jnp.where(valid, j, nj - 1),
        first=(valid & (r == 0)).astype(I32),
        rows=jnp.where(valid, jnp.clip(counts[e] - r * rb, 0, rb), 0).astype(I32),
    )
    return seg, steps


def _aligned(row):
    return row if isinstance(row, int) else pl.multiple_of(row, SEG_ALIGN)


def _segment_copies(cnt, fn):
    for b in range(SEG_ALIGN.bit_length() - 1, SEG_MAX.bit_length()):
        size = 1 << b
        start = (cnt >> (b + 1)) << (b + 1)

        @pl.when(((cnt >> b) & 1) == 1)
        def _(start=start, size=size):
            fn(start, size)


def _dispatch_kernel(start_ref, cnt_ref, pad_ref, rt_ref, h_ref, xs_ref, buf_ref, zero_ref, sem):
    b = pl.program_id(0)
    pos_t = jnp.transpose(rt_ref[...].astype(F32)).astype(I32)
    slot = lax.broadcasted_iota(I32, (SLOT_ROWS, ROUTE_ROWS), 0)
    sel = jnp.zeros((SLOT_ROWS, ROUTE_ROWS), F32)
    for k in range(TOP_K):
        sel += (slot == pos_t[TOP_K + k:TOP_K + k + 1, :]).astype(F32)
    xs = _bdot(sel.astype(BF16), h_ref[...])
    bits = lax.bitcast_convert_type(xs, U32)
    half = D_MODEL // 2
    buf_ref[...] = (bits[:, half:] & jnp.uint32(0xFFFF0000)) | (bits[:, :half] >> 16)

    def copy(src_ref, src, dst, size):
        return pltpu.make_async_copy(src_ref.at[pl.ds(_aligned(src), size)],
                                     xs_ref.at[pl.ds(_aligned(dst), size)], sem)

    def for_segments(fn):
        def body(e, off):
            c = cnt_ref[b * N_EXPERTS + e]
            dst = start_ref[b * N_EXPERTS + e]
            _segment_copies(c, lambda st, size: fn(copy(buf_ref, off + st, dst + st, size)))
            return off + c
        lax.fori_loop(0, N_EXPERTS, body, 0)

    for_segments(lambda cp: cp.start())
    for_segments(lambda cp: cp.wait())

    @pl.when(b == pl.num_programs(0) - 1)
    def _():
        zero_ref[...] = jnp.zeros(zero_ref.shape, zero_ref.dtype)

        def for_pads(fn):
            def body(e, carry):
                dst = pad_ref[e]
                _segment_copies(pad_ref[N_EXPERTS + e], lambda st, size: fn(copy(zero_ref, 0, dst + st, size)))
                return carry
            lax.fori_loop(0, N_EXPERTS, body, 0)

        for_pads(lambda cp: cp.start())
        for_pads(lambda cp: cp.wait())

        tail0 = pad_ref[2 * N_EXPERTS]
        ntail = pad_ref[2 * N_EXPERTS + 1]

        def tail_copy(i):
            return copy(zero_ref, 0, tail0 + i * SEG_MAX, SEG_MAX)

        lax.fori_loop(0, ntail, lambda i, c: (tail_copy(i).start(), c)[1], 0)
        lax.fori_loop(0, ntail, lambda i, c: (tail_copy(i).wait(), c)[1], 0)


def dispatch(h2, rt, seg):
    nb = h2.shape[0] // ROUTE_ROWS
    return pl.pallas_call(
        _dispatch_kernel,
        grid_spec=pltpu.PrefetchScalarGridSpec(
            num_scalar_prefetch=3,
            grid=(nb,),
            in_specs=[pl.BlockSpec((ROUTE_ROWS, 128), lambda b, *_: (b, 0)),
                      pl.BlockSpec((ROUTE_ROWS, D_MODEL), lambda b, *_: (b, 0))],
            out_specs=pl.BlockSpec(memory_space=pl.ANY),
            scratch_shapes=[pltpu.VMEM((SLOT_ROWS, D_MODEL // 2), U32),
                            pltpu.VMEM((SEG_MAX, D_MODEL // 2), U32),
                            pltpu.SemaphoreType.DMA(())],
        ),
        out_shape=jax.ShapeDtypeStruct((N_SLOTS, D_MODEL // 2), U32),
        compiler_params=_cparams("arbitrary"),
        name="dispatch",
    )(seg["start"], seg["cnt"], seg["pad"], rt, h2)


def _unpack_bf16(words):
    lo = lax.bitcast_convert_type(words << 16, F32).astype(BF16)
    hi = lax.bitcast_convert_type(words & jnp.uint32(0xFFFF0000), F32).astype(BF16)
    return lo, hi


def _for_row_counts(rows, o_ref, fn):
    for nc in range(1, EXPERT_ROWS // EXPERT_CHUNK + 1):
        n = nc * EXPERT_CHUNK

        @pl.when((rows + EXPERT_CHUNK - 1) // EXPERT_CHUNK == nc)
        def _(n=n):
            o_ref[:n, :] = fn(n).astype(o_ref.dtype)
            if n < EXPERT_ROWS:
                o_ref[n:, :] = jnp.zeros((EXPERT_ROWS - n, o_ref.shape[1]), o_ref.dtype)

    @pl.when(rows == 0)
    def _():
        o_ref[...] = jnp.zeros(o_ref.shape, o_ref.dtype)


def _expert_up_kernel(xb_ref, ob_ref, oj_ref, we_ref, wj_ref, first_ref, rows_ref,
                      x_ref, wg_ref, wu_ref, bg_ref, bu_ref, o_ref, wgb_ref, wub_ref):
    s = pl.program_id(0)

    @pl.when(first_ref[s] == 1)
    def _():
        wgb_ref[...] = wg_ref[...].astype(BF16)
        wub_ref[...] = wu_ref[...].astype(BF16)

    def act(n):
        lo, hi = _unpack_bf16(x_ref[:n, :])
        half = D_MODEL // 2
        g = _bdot(lo, wgb_ref[:half, :]) + _bdot(hi, wgb_ref[half:, :]) + bg_ref[...]
        u = _bdot(lo, wub_ref[:half, :]) + _bdot(hi, wub_ref[half:, :]) + bu_ref[...]
        g = jnp.minimum(g, SWIGLU_LIMIT)
        u = jnp.clip(u, -SWIGLU_LIMIT, SWIGLU_LIMIT)
        return g * jax.nn.sigmoid(SWIGLU_ALPHA * g) * (u + 1.0)

    _for_row_counts(rows_ref[s], o_ref, act)


def expert_up(layer, xs, w_gu, b_gu, steps):
    tn = EXPERT_TN
    nj = D_EXPERT // tn
    n_steps = N_BLK * nj
    wspec = lambda up: pl.BlockSpec(
        (None, None, D_MODEL, tn), lambda s, xb, ob, oj, we, wj, fi, ro: (layer, we[s], 0, up * nj + wj[s]))
    bspec = lambda up: pl.BlockSpec(
        (None, None, 1, tn), lambda s, xb, ob, oj, we, wj, fi, ro: (layer, we[s], 0, up * nj + wj[s]))
    return pl.pallas_call(
        _expert_up_kernel,
        grid_spec=pltpu.PrefetchScalarGridSpec(
            num_scalar_prefetch=7,
            grid=(n_steps,),
            in_specs=[
                pl.BlockSpec((EXPERT_ROWS, D_MODEL // 2), lambda s, xb, ob, oj, we, wj, fi, ro: (xb[s], 0)),
                wspec(0), wspec(1), bspec(0), bspec(1),
            ],
            out_specs=pl.BlockSpec((EXPERT_ROWS, tn), lambda s, xb, ob, oj, we, wj, fi, ro: (ob[s], oj[s])),
            scratch_shapes=[pltpu.VMEM((D_MODEL, tn), BF16), pltpu.VMEM((D_MODEL, tn), BF16)],
        ),
        out_shape=jax.ShapeDtypeStruct((N_SLOTS, D_EXPERT), BF16),
        compiler_params=_cparams("arbitrary"),
        name="expert_up",
    )(steps["x_blk"], steps["o_blk"], steps["o_j"], steps["w_e"], steps["w_j"], steps["first"],
      steps["rows"], xs, w_gu, w_gu, b_gu, b_gu)


def _expert_down_kernel(xb_ref, ob_ref, oj_ref, we_ref, wj_ref, first_ref, rows_ref,
                        a_ref, w_ref, b_ref, o_ref, wb_ref):
    s = pl.program_id(0)

    @pl.when(first_ref[s] == 1)
    def _():
        wb_ref[...] = w_ref[...].astype(BF16)

    _for_row_counts(rows_ref[s], o_ref, lambda n: _bdot(a_ref[:n, :], wb_ref[...]) + b_ref[...])


def expert_down(layer, act, w_dn, b_dn, steps):
    tn = EXPERT_TN
    nj = D_MODEL // tn
    n_steps = N_BLK * nj
    return pl.pallas_call(
        _expert_down_kernel,
        grid_spec=pltpu.PrefetchScalarGridSpec(
            num_scalar_prefetch=7,
            grid=(n_steps,),
            in_specs=[
                pl.BlockSpec((EXPERT_ROWS, D_EXPERT), lambda s, xb, ob, oj, we, wj, fi, ro: (xb[s], 0)),
                pl.BlockSpec((None, None, D_EXPERT, tn),
                             lambda s, xb, ob, oj, we, wj, fi, ro: (layer, we[s], 0, wj[s])),
                pl.BlockSpec((None, None, 1, tn),
                             lambda s, xb, ob, oj, we, wj, fi, ro: (layer, we[s], 0, wj[s])),
            ],
            out_specs=pl.BlockSpec((EXPERT_ROWS, tn), lambda s, xb, ob, oj, we, wj, fi, ro: (ob[s], oj[s])),
            scratch_shapes=[pltpu.VMEM((D_EXPERT, tn), BF16)],
        ),
        out_shape=jax.ShapeDtypeStruct((N_SLOTS, D_MODEL), F32),
        compiler_params=_cparams("arbitrary"),
        name="expert_down",
    )(steps["x_blk"], steps["o_blk"], steps["o_j"], steps["w_e"], steps["w_j"], steps["first"],
      steps["rows"], act, w_dn, b_dn)


def _combine_kernel(final, start_ref, cnt_ref, rt_ref, gt_ref, x_ref, g2_ref, nf_ref, ys_ref,
                    o_ref, buf_ref, sem):
    b = pl.program_id(0)

    def for_segments(fn):
        def body(e, off):
            c = cnt_ref[b * N_EXPERTS + e]
            src = start_ref[b * N_EXPERTS + e]
            _segment_copies(c, lambda st, size: fn(pltpu.make_async_copy(
                ys_ref.at[pl.ds(_aligned(src + st), size)], buf_ref.at[pl.ds(_aligned(off + st), size)], sem)))
            return off + c
        return lax.fori_loop(0, N_EXPERTS, body, 0)

    tail = ROUTE_ROWS * TOP_K
    buf_ref[tail:, :] = jnp.zeros((SLOT_ROWS - tail, D_MODEL), F32)
    for_segments(lambda cp: cp.start())
    for_segments(lambda cp: cp.wait())

    rt = rt_ref[...]
    gt = gt_ref[...]
    slot = lax.broadcasted_iota(I32, (ROUTE_ROWS, SLOT_ROWS), 1)
    w = jnp.zeros((ROUTE_ROWS, SLOT_ROWS), F32)
    for k in range(TOP_K):
        w += jnp.where(slot == rt[:, TOP_K + k:TOP_K + k + 1], gt[:, k:k + 1], 0.0)
    w_hi = w.astype(BF16)
    w_lo = (w - w_hi.astype(F32)).astype(BF16)
    y = buf_ref[...]
    y_hi = y.astype(BF16)
    y_lo = (y - y_hi.astype(F32)).astype(BF16)
    acc = _bdot(w_hi, y_hi) + (_bdot(w_hi, y_lo) + _bdot(w_lo, y_hi))
    x2 = x_ref[...] + g2_ref[...] * acc
    if final:
        x2 = _rms(x2, nf_ref[...])
    o_ref[...] = x2


def combine(ys, rt, gates, seg, x1, g2, norm_f, final):
    tb = ROUTE_ROWS
    t = x1.shape[0]
    return pl.pallas_call(
        functools.partial(_combine_kernel, final),
        grid_spec=pltpu.PrefetchScalarGridSpec(
            num_scalar_prefetch=2,
            grid=(t // tb,),
            in_specs=[
                pl.BlockSpec((tb, 128), lambda i, *_: (i, 0)),
                pl.BlockSpec((tb, 128), lambda i, *_: (i, 0)),
                pl.BlockSpec((tb, D_MODEL), lambda i, *_: (i, 0)),
                pl.BlockSpec((None, 1, D_MODEL), lambda i, *_: (_mod_row(i, tb), 0, 0)),
                pl.BlockSpec((1, D_MODEL), lambda i, *_: (0, 0)),
                pl.BlockSpec(memory_space=pl.ANY),
            ],
            out_specs=pl.BlockSpec((tb, D_MODEL), lambda i, *_: (i, 0)),
            scratch_shapes=[pltpu.VMEM((SLOT_ROWS, D_MODEL), F32), pltpu.SemaphoreType.DMA(())],
        ),
        out_shape=jax.ShapeDtypeStruct((t, D_MODEL), F32),
        compiler_params=_cparams("arbitrary"),
        name="combine",
    )(seg["start"], seg["cnt"], rt, gates, x1, g2, norm_f, ys)


def _rope_tables():
    rows = DEC_SEQ // GRID_W
    row = jnp.repeat(jnp.arange(rows, dtype=F32), GRID_W)
    col = jnp.tile(jnp.arange(GRID_W, dtype=F32), rows)
    half = QK_ROPE // 2
    inv = 1.0 / (ROPE_THETA ** (jnp.arange(0, half, 2, dtype=F32) / half))
    ar = row[:, None] * inv
    ac = col[:, None] * inv
    ang = jnp.concatenate([ar, ar, ac, ac], axis=-1)
    cos = jnp.concatenate([jnp.ones((SEQ, QK_ROPE), F32), jnp.cos(ang)], axis=0)
    sin = jnp.concatenate([jnp.zeros((SEQ, QK_ROPE), F32), jnp.sin(ang)], axis=0)
    scale = (QK_NOPE + QK_ROPE) ** -0.5
    tab_q = jnp.concatenate([jnp.ones((TAB_ROWS, QK_NOPE), F32), cos, sin], axis=-1) * scale
    tab_k = jnp.concatenate([cos, sin], axis=-1)
    return tab_q, tab_k


def _rotate_cols(w):
    r1, r2, c1, c2 = jnp.split(w, 4, axis=-1)
    return jnp.concatenate([-r2, r1, -c2, c1], axis=-1)


def _layer_weights(w_in, w_uq, w_uk, w_uv, w_out, s5_w_glu, w_router, b_router):
    hy, q, kv, kr, s5 = jnp.split(w_in, [HY_PROJ, HY_PROJ + Q_RANK, HY_PROJ + Q_RANK + KV_RANK,
                                         HY_PROJ + Q_RANK + KV_RANK + QK_ROPE], axis=-1)
    w_in_aug = jnp.concatenate([hy, q, kv, kr, _rotate_cols(kr), s5], axis=-1).astype(BF16)
    wq = w_uq.reshape(Q_RANK, MLA_HEADS, QK_NOPE + QK_ROPE)
    wq_rope = wq[..., QK_NOPE:]
    w_q = jnp.concatenate([wq[..., :QK_NOPE], wq_rope, _rotate_cols(wq_rope)], axis=-1)
    w_q = w_q.reshape(Q_RANK, MLA_HEADS * QH).astype(BF16)
    wk = w_uk.reshape(KV_RANK, MLA_HEADS, QK_NOPE)
    wk = jnp.concatenate([wk, jnp.zeros((KV_RANK, MLA_HEADS, 2 * QK_ROPE), F32)], axis=-1)
    eye = jnp.eye(QK_ROPE, dtype=F32)
    ek = jnp.concatenate([jnp.zeros((QK_ROPE, QK_NOPE), F32), eye, eye], axis=-1)
    ek = jnp.broadcast_to(ek[:, None, :], (QK_ROPE, MLA_HEADS, QH))
    wk = jnp.concatenate([wk, ek, jnp.zeros((QK_ROPE, MLA_HEADS, QH), F32)], axis=0)
    wv = jnp.concatenate([w_uv, jnp.zeros((2 * QK_ROPE, MLA_W), F32)], axis=0)
    w_kv = jnp.concatenate([wk.reshape(KVR_W, MLA_HEADS * QH), wv], axis=-1).astype(BF16)
    w_r = jnp.concatenate([w_router, jnp.zeros((D_MODEL, 128 - N_EXPERTS), F32)], axis=-1)
    b_r = jnp.concatenate([b_router, jnp.zeros((128 - N_EXPERTS,), F32)])[None, :]
    w_r_hi, w_r_lo = _split_bf16(w_r)
    return w_in_aug, w_q, w_kv, w_out.astype(BF16), s5_w_glu.astype(BF16), w_r_hi, w_r_lo, b_r


def _s5_block_diag(lbr, lbi, bbr, bbi, c_re, c_im):
    eye = jnp.eye(S5_GROUPS, dtype=F32)

    def bd_in(b):
        return jnp.einsum("dgpn,gh->dgnhp", b, eye).reshape(2, S5_W, S5_NS).astype(BF16)

    def bd_out(c):
        return jnp.einsum("dgnp,gh->dgphn", c, eye).reshape(2, S5_NS, S5_W).astype(BF16)

    lam = jnp.concatenate([lbr.reshape(2, 1, S5_NS), lbi.reshape(2, 1, S5_NS)], axis=-1)
    lam8 = jnp.broadcast_to(lam, (2, S5_SUB, 2 * S5_NS))
    return bd_in(bbr), bd_in(bbi), bd_out(c_re), bd_out(c_im), lam8


def _time_major(z, nb, n, bp):
    u = z.reshape(nb, n, S5_W).transpose(1, 0, 2)
    if bp > nb:
        u = jnp.concatenate([u, jnp.zeros((n, bp - nb, S5_W), u.dtype)], axis=1)
    return u


def kernel(x_prompt, x_sample, cache_ckv, cache_krope, state_ssm, c, c_ctx, w_ada, b_ada, norm1, norm2, w_in, w_out, hy_conv_w, hy_conv_b, hy_fw1, hy_fb1, hy_fw2, hy_fb2, hy_freq, hy_fw3, hy_decay, hy_skip, q_norm, kv_norm, w_uq, w_uk, w_uv, s5_lam_re, s5_lam_im, s5_log_dt, s5_b_re, s5_b_im, s5_c_re, s5_c_im, s5_d, s5_w_glu, s5_b_glu, w_router, b_router, w_gate_up, b_gate_up, w_down, b_down, norm_f):
    x = jnp.concatenate([x_prompt.reshape(T_P, D_MODEL), x_sample.reshape(T_S, D_MODEL)], axis=0)

    cond8 = jnp.concatenate([c_ctx[None, :], c, jnp.zeros((N_MOD - 1 - DEC_BATCH, D_MODEL), F32)], axis=0)
    mod = ada_modulation(cond8, w_ada, b_ada).reshape(DEPTH, N_MOD, 6, 1, D_MODEL)

    tab_q, tab_k = _rope_tables()
    dft = {}
    for n in (SEQ, DEC_SEQ):
        fwd, inv = _dft_tables(n)
        f_hi, f_lo = _split_bf16(fwd)
        dft[n] = (f_hi, f_lo, jnp.asarray(inv, F32).astype(BF16))
    fw1p = jnp.concatenate([hy_fw1, jnp.zeros((DEPTH, FILT_HID - POS_EMB, FILT_HID), F32)], axis=1)
    spectra = {n: hyena_spectra(n, fw1p, hy_fb1[:, None, :], hy_fw2, hy_fb2[:, None, :], hy_freq,
                                hy_fw3, hy_decay[:, None, :], dft[n][0], dft[n][1])
               for n in (SEQ, DEC_SEQ)}

    lbr, lbi, bbr, bbi = s5_params(s5_lam_re, s5_lam_im, s5_log_dt, s5_b_re, s5_b_im)
    b_gu4 = b_gate_up.reshape(DEPTH, N_EXPERTS, 1, 2 * D_EXPERT)
    b_dn4 = b_down.reshape(DEPTH, N_EXPERTS, 1, D_MODEL)
    bp_s = S5_SUB

    ckv_out, kr_out, ssm_out = [], [], []
    for l in range(DEPTH):
        w_in_aug, w_q, w_kv, w_out_b, w_glu_b, w_r_hi, w_r_lo, b_r = _layer_weights(
            w_in[l], w_uq[l], w_uk[l], w_uv[l], w_out[l], s5_w_glu[l], w_router[l], b_router[l])
        m = lambda k: mod[l, :, k]
        z_hy, z_q, z_kvr, z_s5 = in_proj(x, norm1[l][None, :], m(1), m(0), w_in_aug)

        cw, cb, sk = hy_conv_w[l], hy_conv_b[l][None, :], hy_skip[l]
        y_hy = hyena(z_hy, cw, cb, dft[SEQ][0], dft[SEQ][2], spectra[SEQ][l], sk,
                     n=SEQ, nbatch=BATCH, row0=0, nseq=8, t_out=T_ALL,
                     prev=jnp.zeros((T_ALL, HY_W), BF16))
        y_hy = hyena(z_hy, cw, cb, dft[DEC_SEQ][0], dft[DEC_SEQ][2], spectra[DEC_SEQ][l], sk,
                     n=DEC_SEQ, nbatch=DEC_BATCH, row0=T_P, nseq=1, t_out=T_ALL, prev=y_hy)

        q = q_proj(z_q, q_norm[l][None, :], w_q, tab_q)
        a = kv_prep(z_kvr, kv_norm[l][None, :], tab_k)
        a_p = a[:T_P]
        ckv_out.append(a_p[:, :KV_RANK].reshape(BATCH, SEQ, KV_RANK))
        kr_out.append(a_p[:, KV_RANK:KV_RANK + QK_ROPE].reshape(BATCH, SEQ, QK_ROPE))
        ctx = jnp.concatenate([cache_ckv[:, l], cache_krope[:, l],
                               jnp.zeros((DEC_BATCH, PAST_LEN, QK_ROPE), F32)], axis=-1)
        a_s = jnp.concatenate([a[T_P:].reshape(DEC_BATCH, DEC_SEQ, KVR_W), ctx], axis=1)
        lk_s = DEC_SEQ + PAST_LEN
        k_p, v_p = kv_up(a_p, w_kv)
        k_s, v_s = kv_up(a_s.reshape(DEC_BATCH * lk_s, KVR_W), w_kv)
        y_att = attention(q, k_p, v_p, nbatch=BATCH, lq=SEQ, lk=SEQ, row0=0, t_out=T_ALL,
                          prev=jnp.zeros((T_ALL, MLA_W), BF16))
        y_att = attention(q, k_s, v_s, nbatch=DEC_BATCH, lq=DEC_SEQ, lk=lk_s, row0=T_P, t_out=T_ALL,
                          prev=y_att)

        b_re, b_im, c_re, c_im, lam8 = _s5_block_diag(lbr[l], lbi[l], bbr[l], bbi[l],
                                                      s5_c_re[l], s5_c_im[l])
        u_p = _time_major(z_s5[:T_P], BATCH, SEQ, BATCH)
        u_s = _time_major(z_s5[T_P:], DEC_BATCH, DEC_SEQ, bp_s)
        st = state_ssm[:, l].reshape(DEC_BATCH, 2, 2 * S5_NS).transpose(1, 0, 2)
        x0_s = jnp.concatenate([st, jnp.zeros((2, bp_s - DEC_BATCH, 2 * S5_NS), F32)], axis=1)
        x0_p = jnp.zeros((2, BATCH, 2 * S5_NS), F32)
        y2_p, xf_p = s5_scan(u_p, b_re, b_im, c_re, c_im, lam8, x0_p)
        y2_s, _ = s5_scan(u_s, b_re, b_im, c_re, c_im, lam8, x0_s)
        ssm_out.append(xf_p.reshape(2, BATCH, 2, S5_GROUPS, S5_STATE).transpose(1, 0, 2, 3, 4))
        d_row, bg_row = s5_d[l][None, :], s5_b_glu[l][None, :]
        ys_p = s5_glu(y2_p.reshape(2, SEQ * BATCH, S5_W), u_p.reshape(SEQ * BATCH, S5_W),
                      d_row, w_glu_b, bg_row)
        ys_s = s5_glu(y2_s.reshape(2, DEC_SEQ * bp_s, S5_W), u_s.reshape(DEC_SEQ * bp_s, S5_W),
                      d_row, w_glu_b, bg_row)
        y_s5 = jnp.concatenate([
            ys_p.reshape(SEQ, BATCH, S5_W).transpose(1, 0, 2).reshape(T_P, S5_W),
            ys_s.reshape(DEC_SEQ, bp_s, S5_W)[:, :DEC_BATCH].transpose(1, 0, 2).reshape(T_S, S5_W)], axis=0)

        x1, h2, rt, gates, cnt = out_proj(y_hy, y_att, y_s5, w_out_b, x, m(2), norm2[l][None, :],
                                          m(4), m(3), w_r_hi, w_r_lo, b_r)

        seg, steps = route_tables(cnt)
        xs = dispatch(h2, rt, seg)
        act = expert_up(l, xs, w_gate_up, b_gu4, steps)
        ys = expert_down(l, act, w_down, b_dn4, steps)
        x = combine(ys, rt, gates, seg, x1, m(5), norm_f[None, :], final=(l == DEPTH - 1))

    y_prompt = x[:T_P].reshape(BATCH, SEQ, D_MODEL)
    y_sample = x[T_P:].reshape(DEC_BATCH, DEC_SEQ, D_MODEL)
    return (y_prompt, y_sample, jnp.stack(ckv_out, axis=1), jnp.stack(kr_out, axis=1),
            jnp.stack(ssm_out, axis=1))
```

```python
import functools
import math

import numpy as np
import jax
import jax.numpy as jnp
from jax import lax
from jax.experimental import pallas as pl
from jax.experimental.pallas import tpu as pltpu

F32 = jnp.float32
BF16 = jnp.bfloat16
I32 = jnp.int32
U32 = jnp.uint32
HIGHEST = lax.Precision.HIGHEST

D_MODEL = 2048
BATCH = 32
SEQ = 256
DEPTH = 4
DEC_BATCH = 4
DEC_SEQ = 1024
PAST_LEN = 512
GRID_W = 64
EPS = 1e-6
HY_W = 512
HY_PROJ = 3 * HY_W
POS_BANDS = 16
POS_EMB = 2 * POS_BANDS + 1
FILT_HID = 64
MLA_HEADS = 8
Q_RANK = 512
KV_RANK = 256
QK_NOPE = 128
QK_ROPE = 64
V_DIM = 128
MLA_W = MLA_HEADS * V_DIM
ROPE_THETA = 10000.0
S5_W = 512
S5_GROUP_CH = 16
S5_GROUPS = 32
S5_STATE = 64
S5_NS = S5_GROUPS * S5_STATE
N_EXPERTS = 32
TOP_K = 4
D_EXPERT = D_MODEL
SWIGLU_LIMIT = 7.0
SWIGLU_ALPHA = 1.702

T_P = BATCH * SEQ
T_S = DEC_BATCH * DEC_SEQ
T_ALL = T_P + T_S
N_MOD = 8
QH = 256
KVR_W = KV_RANK + 2 * QK_ROPE
ROW_TILE = 256
TAB_ROWS = SEQ + DEC_SEQ

VMEM_LIMIT_BYTES = 56 * 1024 * 1024

EXPERT_ROWS = 512
N_ASSIGN = T_ALL * TOP_K
ROUTE_ROWS = 256
N_RBLK = T_ALL // ROUTE_ROWS
SEG_ALIGN = 8
SEG_MAX = ROUTE_ROWS
SLOT_ROWS = ROUTE_ROWS * TOP_K + N_EXPERTS * SEG_ALIGN
N_BLK = (N_ASSIGN + N_RBLK * N_EXPERTS * (SEG_ALIGN - 1)) // EXPERT_ROWS + 1 + N_EXPERTS
N_SLOTS = N_BLK * EXPERT_ROWS
EXPERT_TN = 1024
EXPERT_CHUNK = 128
MM_ROWS = 512


def _cparams(*sem):
    return pltpu.CompilerParams(dimension_semantics=sem, vmem_limit_bytes=VMEM_LIMIT_BYTES)


def _mod_row(i, tm):
    npb = T_P // tm
    sb = DEC_SEQ // tm
    return jnp.where(i < npb, 0, 1 + (i - npb) // sb)


def _tab_blk(i, tm):
    npb = T_P // tm
    pb = SEQ // tm
    sb = DEC_SEQ // tm
    return jnp.where(i < npb, i % pb, pb + (i - npb) % sb)


def _rms(x, g):
    return x * lax.rsqrt(jnp.mean(x * x, axis=-1, keepdims=True) + EPS) * g


def _bdot(a, b):
    return jnp.dot(a, b, preferred_element_type=F32)


def _ada_kernel(c_ref, w_ref, b_ref, o_ref):
    s = jax.nn.silu(c_ref[...]).astype(BF16)
    o_ref[...] = _bdot(s, w_ref[...].astype(BF16)) + b_ref[...]


def ada_modulation(cond8, w_ada, b_ada):
    tn = 1024
    n = 6 * D_MODEL
    return pl.pallas_call(
        _ada_kernel,
        grid=(DEPTH, n // tn),
        in_specs=[
            pl.BlockSpec((N_MOD, D_MODEL), lambda l, j: (0, 0)),
            pl.BlockSpec((None, D_MODEL, tn), lambda l, j: (l, 0, j)),
            pl.BlockSpec((None, 1, tn), lambda l, j: (l, 0, j)),
        ],
        out_specs=pl.BlockSpec((None, N_MOD, tn), lambda l, j: (l, 0, j)),
        out_shape=jax.ShapeDtypeStruct((DEPTH, N_MOD, n), F32),
        compiler_params=_cparams("arbitrary", "arbitrary"),
        name="ada_modulation",
    )(cond8, w_ada, b_ada.reshape(DEPTH, 1, n))


IN_COLS = (HY_PROJ, Q_RANK, KVR_W, S5_W)
IN_W_AUG = sum(IN_COLS)


def _in_proj_kernel(x_ref, g_ref, sc_ref, sh_ref, w_ref, zhy_ref, zq_ref, zkvr_ref, zs5_ref):
    h = _rms(x_ref[...], g_ref[...]) * (1.0 + sc_ref[...]) + sh_ref[...]
    h = h.astype(BF16)
    off = 0
    for o_ref, wd in zip((zhy_ref, zq_ref, zkvr_ref, zs5_ref), IN_COLS):
        o_ref[...] = _bdot(h, w_ref[:, off:off + wd])
        off += wd


def in_proj(x, g, sc, sh, w_aug):
    tm = MM_ROWS
    t = x.shape[0]
    mod_spec = pl.BlockSpec((None, 1, D_MODEL), lambda i: (_mod_row(i, tm), 0, 0))
    return pl.pallas_call(
        _in_proj_kernel,
        grid=(t // tm,),
        in_specs=[
            pl.BlockSpec((tm, D_MODEL), lambda i: (i, 0)),
            pl.BlockSpec((1, D_MODEL), lambda i: (0, 0)),
            mod_spec, mod_spec,
            pl.BlockSpec((D_MODEL, IN_W_AUG), lambda i: (0, 0)),
        ],
        out_specs=[pl.BlockSpec((tm, wd), lambda i: (i, 0)) for wd in IN_COLS],
        out_shape=[jax.ShapeDtypeStruct((t, wd), F32) for wd in IN_COLS],
        compiler_params=_cparams("arbitrary"),
        name="in_proj",
    )(x, g, sc, sh, w_aug)


def _dft_tables(n):
    f = np.arange(n)[:, None]
    s = np.arange(n)[None, :]
    ang = (np.pi / n) * ((f * s) % (2 * n)).astype(np.float64)
    cos = np.cos(ang)
    sin = np.sin(ang)
    nyq = np.where(np.arange(n) % 2 == 0, 1.0, -1.0)
    fs = sin.copy()
    fs[0, :] = nyq
    fwd = np.concatenate([cos, fs], axis=0)
    wgt = np.full((n,), 2.0)
    wgt[0] = 1.0
    gc = (cos * wgt[:, None]).T / (2 * n)
    gs = (sin * wgt[:, None]).T / (2 * n)
    gs[:, 0] = nyq / (2 * n)
    inv = np.concatenate([gc, gs], axis=1)
    return fwd, inv


def _split_bf16(a):
    hi = jnp.asarray(a, F32).astype(BF16)
    lo = (jnp.asarray(a, F32) - hi.astype(F32)).astype(BF16)
    return hi, lo


def _hyena_spec_kernel(n, feat_ref, t_ref, fw1_ref, fb1_ref, fw2_ref, fb2_ref, freq_ref,
                       w3f0_ref, w3f1_ref, w3b0_ref, w3b1_ref,
                       dcf0_ref, dcf1_ref, dcb0_ref, dcb1_ref,
                       fhi_ref, flo_ref, o_ref):
    hdot = functools.partial(jnp.dot, precision=HIGHEST, preferred_element_type=F32)
    h = jnp.sin(freq_ref[0:1, :] * (hdot(feat_ref[...], fw1_ref[...]) + fb1_ref[...]))
    h = jnp.sin(freq_ref[1:2, :] * (hdot(h, fw2_ref[...]) + fb2_ref[...]))
    tcol = t_ref[...]
    row0 = lax.broadcasted_iota(I32, (n, 1), 0) == 0

    def filt(w3_ref, dc_ref):
        return hdot(h, w3_ref[...]) * jnp.exp(-tcol * jnp.abs(dc_ref[...]))

    def dft(fpart_hi, fpart_lo, k):
        k_hi = k.astype(BF16)
        k_lo = (k - k_hi.astype(F32)).astype(BF16)
        return _bdot(fpart_hi, k_hi) + (_bdot(fpart_hi, k_lo) + _bdot(fpart_lo, k_hi))

    for o, (wf, wb, df, db) in enumerate(((w3f0_ref, w3b0_ref, dcf0_ref, dcb0_ref),
                                          (w3f1_ref, w3b1_ref, dcf1_ref, dcb1_ref))):
        fwd = filt(wf, df)
        bwd = jnp.where(row0, 0.0, filt(wb, db))
        ks = dft(fhi_ref[...], flo_ref[...], fwd + bwd)
        kd = dft(fhi_ref[n:, :], flo_ref[n:, :], fwd - bwd)
        kr = ks[:n]
        ksp = jnp.where(row0, ks[n:n + 1], kd)
        o_ref[3 * o + 0] = kr
        o_ref[3 * o + 1] = jnp.where(row0, 0.0, ksp)
        o_ref[3 * o + 2] = jnp.where(row0, ksp, kr)


def hyena_spectra(n, fw1p, fb1, fw2, fb2, freq, fw3, decay, fwd_hi, fwd_lo):
    tc = 256
    nc = HY_W // tc
    f32 = F32
    t = jnp.linspace(0.0, 1.0, n, dtype=f32)[:, None]
    w = (2.0 * math.pi / n) * jnp.arange(n, dtype=f32)[:, None]
    bands = jnp.linspace(1e-4, POS_BANDS - 1, POS_BANDS, dtype=f32)
    feat = jnp.concatenate([t, jnp.cos(w * bands), -jnp.sin(w * bands),
                            jnp.zeros((n, FILT_HID - POS_EMB), f32)], axis=-1)

    def w3_spec(d, o):
        return pl.BlockSpec((None, FILT_HID, tc), lambda l, c: (l, 0, (d * 2 + o) * nc + c))

    def dc_spec(d, o):
        return pl.BlockSpec((None, 1, tc), lambda l, c: (l, 0, (d * 2 + o) * nc + c))

    const2 = lambda shape: pl.BlockSpec(shape, lambda l, c: (0, 0))
    lay3 = lambda a, b: pl.BlockSpec((None, a, b), lambda l, c: (l, 0, 0))
    return pl.pallas_call(
        functools.partial(_hyena_spec_kernel, n),
        grid=(DEPTH, nc),
        in_specs=[
            const2((n, FILT_HID)), const2((n, 1)),
            lay3(FILT_HID, FILT_HID), lay3(1, FILT_HID), lay3(FILT_HID, FILT_HID), lay3(1, FILT_HID),
            lay3(2, FILT_HID),
            w3_spec(0, 0), w3_spec(0, 1), w3_spec(1, 0), w3_spec(1, 1),
            dc_spec(0, 0), dc_spec(0, 1), dc_spec(1, 0), dc_spec(1, 1),
            const2((2 * n, n)), const2((2 * n, n)),
        ],
        out_specs=pl.BlockSpec((None, 6, n, tc), lambda l, c: (l, 0, 0, c)),
        out_shape=jax.ShapeDtypeStruct((DEPTH, 6, n, HY_W), F32),
        compiler_params=_cparams("arbitrary", "arbitrary"),
        name=f"hyena_spectra_{n}",
    )(feat, t, fw1p, fb1, fw2, fb2, freq, fw3, fw3, fw3, fw3, decay, decay, decay, decay,
      fwd_hi, fwd_lo)


def _hyena_kernel(n, nseq, zv_ref, z1_ref, z2_ref, wv_ref, w1_ref, w2_ref, bv_ref, b1_ref, b2_ref,
                  f_ref, g_ref, spec_ref, skip_ref, *rest):
    o_ref = rest[-1]
    row = lax.broadcasted_iota(I32, (n, 1), 0)
    first = row == 0
    last = row == n - 1

    def conv3(z, w_ref, b_ref):
        zm = jnp.where(first, 0.0, pltpu.roll(z, 1, 0))
        zp = jnp.where(last, 0.0, pltpu.roll(z, n - 1, 0))
        return zm * w_ref[0:1, :] + z * w_ref[1:2, :] + zp * w_ref[2:3, :] + b_ref[...]

    def long_conv(u, o):
        uu = _bdot(f_ref[...], u.astype(BF16))
        a = uu[:n]
        b = uu[n:]
        kr = spec_ref[3 * o + 0]
        ks = spec_ref[3 * o + 1]
        dd = spec_ref[3 * o + 2]
        yre = (kr * a - ks * b).astype(BF16)
        zz = (dd * b + ks * a).astype(BF16)
        y = _bdot(g_ref[:, :n], yre) + _bdot(g_ref[:, n:], zz)
        return y + u * skip_ref[o:o + 1, :]

    for s in range(nseq):
        rows = pl.ds(s * n, n)
        v = conv3(zv_ref[rows, :], wv_ref, bv_ref)
        x1 = conv3(z1_ref[rows, :], w1_ref, b1_ref)
        x2 = conv3(z2_ref[rows, :], w2_ref, b2_ref)
        y = x1 * long_conv(v, 0)
        y = x2 * long_conv(y, 1)
        o_ref[rows, :] = y.astype(o_ref.dtype)


def hyena(z_hy, conv_w, conv_b, f_hi, g_inv, spec, skip, *, n, nbatch, row0, nseq, t_out, prev=None):
    tc = 256
    nc = HY_W // tc
    rb = nseq * n
    blk0 = row0 // rb
    assert row0 % rb == 0 and nbatch % nseq == 0

    def zspec(part):
        return pl.BlockSpec((rb, tc), lambda c, b: (blk0 + b, part * nc + c))

    def wspec(part, rows):
        return pl.BlockSpec((rows, tc), lambda c, b: (0, part * nc + c))

    in_specs = [zspec(0), zspec(1), zspec(2), wspec(0, 3), wspec(1, 3), wspec(2, 3),
                wspec(0, 1), wspec(1, 1), wspec(2, 1),
                pl.BlockSpec((2 * n, n), lambda c, b: (0, 0)),
                pl.BlockSpec((n, 2 * n), lambda c, b: (0, 0)),
                pl.BlockSpec((6, n, tc), lambda c, b: (0, 0, c)),
                pl.BlockSpec((2, tc), lambda c, b: (0, c))]
    args = [z_hy, z_hy, z_hy, conv_w, conv_w, conv_w, conv_b, conv_b, conv_b, f_hi, g_inv, spec, skip]
    aliases = {}
    if prev is not None:
        in_specs.append(pl.BlockSpec(memory_space=pl.ANY))
        args.append(prev)
        aliases = {len(args) - 1: 0}
    return pl.pallas_call(
        functools.partial(_hyena_kernel, n, nseq),
        grid=(nc, nbatch // nseq),
        in_specs=in_specs,
        out_specs=pl.BlockSpec((rb, tc), lambda c, b: (blk0 + b, c)),
        out_shape=jax.ShapeDtypeStruct((t_out, HY_W), BF16),
        input_output_aliases=aliases,
        compiler_params=_cparams("arbitrary", "arbitrary"),
        name=f"hyena_{n}",
    )(*args)


def _q_proj_kernel(z_ref, g_ref, w_ref, tab_ref, o_ref):
    h = _rms(z_ref[...], g_ref[...]).astype(BF16)
    tab = tab_ref[...]
    for hd in range(MLA_HEADS):
        cols = slice(hd * QH, (hd + 1) * QH)
        o_ref[:, cols] = (_bdot(h, w_ref[:, cols]) * tab).astype(BF16)


def q_proj(z_q, g, w_q, tab_q):
    tm = ROW_TILE
    t = z_q.shape[0]
    return pl.pallas_call(
        _q_proj_kernel,
        grid=(t // tm,),
        in_specs=[
            pl.BlockSpec((tm, Q_RANK), lambda i: (i, 0)),
            pl.BlockSpec((1, Q_RANK), lambda i: (0, 0)),
            pl.BlockSpec((Q_RANK, MLA_HEADS * QH), lambda i: (0, 0)),
            pl.BlockSpec((tm, QH), lambda i: (_tab_blk(i, tm), 0)),
        ],
        out_specs=pl.BlockSpec((tm, MLA_HEADS * QH), lambda i: (i, 0)),
        out_shape=jax.ShapeDtypeStruct((t, MLA_HEADS * QH), BF16),
        compiler_params=_cparams("arbitrary"),
        name="q_proj",
    )(z_q, g, w_q, tab_q)


def _kv_prep_kernel(z_ref, g_ref, tab_ref, o_ref):
    o_ref[:, :KV_RANK] = _rms(z_ref[:, :KV_RANK], g_ref[...])
    t = z_ref[:, KV_RANK:] * tab_ref[...]
    o_ref[:, KV_RANK:] = t + pltpu.roll(t, QK_ROPE, 1)


def kv_prep(z_kvr, g, tab_k):
    tm = ROW_TILE
    t = z_kvr.shape[0]
    return pl.pallas_call(
        _kv_prep_kernel,
        grid=(t // tm,),
        in_specs=[
            pl.BlockSpec((tm, KVR_W), lambda i: (i, 0)),
            pl.BlockSpec((1, KV_RANK), lambda i: (0, 0)),
            pl.BlockSpec((tm, 2 * QK_ROPE), lambda i: (_tab_blk(i, tm), 0)),
        ],
        out_specs=pl.BlockSpec((tm, KVR_W), lambda i: (i, 0)),
        out_shape=jax.ShapeDtypeStruct((t, KVR_W), F32),
        compiler_params=_cparams("arbitrary"),
        name="kv_prep",
    )(z_kvr, g, tab_k)


def _kv_up_kernel(a_ref, w_ref, k_ref, v_ref):
    a = a_ref[...].astype(BF16)
    nk = MLA_HEADS * QH
    k_ref[...] = _bdot(a, w_ref[:, :nk]).astype(BF16)
    v_ref[...] = _bdot(a, w_ref[:, nk:]).astype(BF16)


def kv_up(a, w_kv):
    tm = 512
    t = a.shape[0]
    nk = MLA_HEADS * QH
    return pl.pallas_call(
        _kv_up_kernel,
        grid=(t // tm,),
        in_specs=[
            pl.BlockSpec((tm, KVR_W), lambda i: (i, 0)),
            pl.BlockSpec((KVR_W, nk + MLA_W), lambda i: (0, 0)),
        ],
        out_specs=[pl.BlockSpec((tm, nk), lambda i: (i, 0)),
                   pl.BlockSpec((tm, MLA_W), lambda i: (i, 0))],
        out_shape=[jax.ShapeDtypeStruct((t, nk), BF16), jax.ShapeDtypeStruct((t, MLA_W), BF16)],
        compiler_params=_cparams("arbitrary"),
        name="kv_up",
    )(a, w_kv)


def _attn_kernel(q_ref, k_ref, v_ref, *rest):
    o_ref = rest[-1]
    for hd in range(MLA_HEADS):
        q = q_ref[:, hd * QH:(hd + 1) * QH]
        k = k_ref[:, hd * QH:(hd + 1) * QH]
        s = lax.dot_general(q, k, (((1,), (1,)), ((), ())), preferred_element_type=F32)
        m = jnp.max(s, axis=-1, keepdims=True)
        p = jnp.exp(s - m)
        l = jnp.sum(p, axis=-1, keepdims=True)
        o = _bdot(p.astype(BF16), v_ref[:, hd * V_DIM:(hd + 1) * V_DIM])
        o_ref[:, hd * V_DIM:(hd + 1) * V_DIM] = (o / l).astype(o_ref.dtype)


def attention(q, k, v, *, nbatch, lq, lk, row0, t_out, prev=None):
    tq = ROW_TILE
    nq = lq // tq
    blk0 = row0 // tq
    in_specs = [
        pl.BlockSpec((tq, MLA_HEADS * QH), lambda b, i: (blk0 + b * nq + i, 0)),
        pl.BlockSpec((lk, MLA_HEADS * QH), lambda b, i: (b, 0)),
        pl.BlockSpec((lk, MLA_W), lambda b, i: (b, 0)),
    ]
    args = [q, k, v]
    aliases = {}
    if prev is not None:
        in_specs.append(pl.BlockSpec(memory_space=pl.ANY))
        args.append(prev)
        aliases = {3: 0}
    return pl.pallas_call(
        _attn_kernel,
        grid=(nbatch, nq),
        in_specs=in_specs,
        out_specs=pl.BlockSpec((tq, MLA_W), lambda b, i: (blk0 + b * nq + i, 0)),
        out_shape=jax.ShapeDtypeStruct((t_out, MLA_W), BF16),
        input_output_aliases=aliases,
        compiler_params=_cparams("arbitrary", "arbitrary"),
        name=f"attention_{lk}",
    )(*args)


def _s5_param_kernel(lr_ref, li_ref, dt_ref, lrx_ref, lix_ref, dtx_ref, bre_ref, bim_ref,
                     lbr_ref, lbi_ref, bbr_ref, bbi_ref):
    def lam_bar(lr, li, ldt):
        lr = jnp.minimum(lr, -1e-4)
        dt = jnp.exp(ldt)
        e = jnp.exp(lr * dt)
        return lr, e * jnp.cos(li * dt), e * jnp.sin(li * dt)

    _, lbr, lbi = lam_bar(lr_ref[...], li_ref[...], dt_ref[...])
    lbr_ref[...] = lbr
    lbi_ref[...] = lbi
    lr, xr, xi = lam_bar(lrx_ref[...], lix_ref[...], dtx_ref[...])
    li = lix_ref[...]
    nr = xr - 1.0
    den = lr * lr + li * li
    cr = (nr * lr + xi * li) / den
    ci = (xi * lr - nr * li) / den
    bbr_ref[...] = cr * bre_ref[...] - ci * bim_ref[...]
    bbi_ref[...] = cr * bim_ref[...] + ci * bre_ref[...]


def s5_params(lam_re, lam_im, log_dt, b_re, b_im):
    r = DEPTH * 2 * S5_GROUPS
    ldt = jnp.broadcast_to(log_dt[..., None], lam_re.shape)
    small = [a.reshape(r, S5_STATE) for a in (lam_re, lam_im, ldt)]
    wide = [jnp.repeat(a, S5_GROUP_CH, axis=-1) for a in small]
    bs = [a.reshape(r, S5_STATE * S5_GROUP_CH) for a in (b_re, b_im)]
    outs = pl.pallas_call(
        _s5_param_kernel,
        out_shape=[jax.ShapeDtypeStruct((r, S5_STATE), F32)] * 2
        + [jax.ShapeDtypeStruct((r, S5_STATE * S5_GROUP_CH), F32)] * 2,
        name="s5_params",
    )(*small, *wide, *bs)
    lbr, lbi, bbr, bbi = outs
    shp = (DEPTH, 2, S5_GROUPS, S5_STATE)
    return (lbr.reshape(shp), lbi.reshape(shp),
            bbr.reshape(shp + (S5_GROUP_CH,)), bbi.reshape(shp + (S5_GROUP_CH,)))


S5_SUB = 8
S5_LANES = 512
S5_CH = S5_LANES // S5_STATE * S5_GROUP_CH


def _s5_scan_kernel(tq, nseq, u_ref, pin_ref, pout_ref, bre_ref, bim_ref, cre_ref, cim_ref, lam_ref, x0_ref,
                    y_ref, xf_ref, s_ref, x_ref):
    d = pl.program_id(0)
    j = pl.program_id(2)

    @pl.when(j == 0)
    def _():
        x_ref[...] = x0_ref[...]

    u = u_ref[...].reshape(nseq * tq, S5_W).astype(BF16)
    u = _bdot(pin_ref[...], u).astype(BF16)
    for c in range(S5_NS // S5_LANES):
        ch = slice(c * S5_CH, (c + 1) * S5_CH)
        st = slice(c * S5_LANES, (c + 1) * S5_LANES)
        s_ref[:, st] = _bdot(u[:, ch], bre_ref[ch, st])
        s_ref[:, S5_NS + c * S5_LANES:S5_NS + (c + 1) * S5_LANES] = _bdot(u[:, ch], bim_ref[ch, st])

    for c in range(S5_NS // S5_LANES):
        re = pl.ds(c * S5_LANES, S5_LANES)
        im = pl.ds(S5_NS + c * S5_LANES, S5_LANES)
        lr = lam_ref[:, re]
        li = lam_ref[:, im]

        def step(i, carry):
            xr, xi = carry
            t = jnp.where(d == 0, i, tq - 1 - i)
            r = pl.ds(pl.multiple_of(t * S5_SUB, S5_SUB), S5_SUB)
            nr = lr * xr - li * xi + s_ref[r, re]
            ni = lr * xi + li * xr + s_ref[r, im]
            s_ref[r, re] = nr
            s_ref[r, im] = ni
            return nr, ni

        xr, xi = lax.fori_loop(0, tq, step, (x_ref[:, re], x_ref[:, im]))
        x_ref[:, re] = xr
        x_ref[:, im] = xi

    for c in range(S5_NS // S5_LANES):
        ch = slice(c * S5_CH, (c + 1) * S5_CH)
        st = slice(c * S5_LANES, (c + 1) * S5_LANES)
        y = (_bdot(s_ref[:, st].astype(BF16), cre_ref[st, ch])
             - _bdot(s_ref[:, S5_NS + c * S5_LANES:S5_NS + (c + 1) * S5_LANES].astype(BF16), cim_ref[st, ch]))
        y_hi = y.astype(BF16)
        y_lo = (y - y_hi.astype(F32)).astype(BF16)
        y = _bdot(pout_ref[...], y_hi) + _bdot(pout_ref[...], y_lo)
        y_ref[:, :, ch] = y.reshape(nseq, tq, S5_CH)

    @pl.when(j == pl.num_programs(2) - 1)
    def _():
        xf_ref[...] = x_ref[...]


def s5_scan(z_s5, b_re, b_im, c_re, c_im, lam8, x0, *, nbatch, n, row0):
    tq = 64
    nch = n // tq
    nseq = min(nbatch, S5_SUB)
    u = z_s5.reshape(-1, n, S5_W)
    g0 = row0 // n // nseq
    assert row0 % (n * nseq) == 0 and nbatch % nseq == 0
    t_i, b_i = np.meshgrid(np.arange(tq), np.arange(nseq), indexing="ij")
    pin = np.zeros((tq * S5_SUB, nseq * tq), np.float32)
    pin[(t_i * S5_SUB + b_i).ravel(), (b_i * tq + t_i).ravel()] = 1.0
    pin_b = jnp.asarray(pin, BF16)
    pout_b = jnp.asarray(pin.T, BF16)

    def tci(d, j):
        return d * (nch - 1) + (1 - 2 * d) * j

    dir3 = lambda a, b: pl.BlockSpec((None, a, b), lambda d, g, j: (d, 0, 0))
    const = lambda a, b: pl.BlockSpec((a, b), lambda d, g, j: (0, 0))
    return pl.pallas_call(
        functools.partial(_s5_scan_kernel, tq, nseq),
        grid=(2, nbatch // nseq, nch),
        in_specs=[
            pl.BlockSpec((nseq, tq, S5_W), lambda d, g, j: (g0 + g, tci(d, j), 0)),
            const(tq * S5_SUB, nseq * tq), const(nseq * tq, tq * S5_SUB),
            dir3(S5_W, S5_NS), dir3(S5_W, S5_NS), dir3(S5_NS, S5_W), dir3(S5_NS, S5_W),
            dir3(S5_SUB, 2 * S5_NS),
            pl.BlockSpec((None, S5_SUB, 2 * S5_NS), lambda d, g, j: (d, g, 0)),
        ],
        out_specs=[
            pl.BlockSpec((None, nseq, tq, S5_W), lambda d, g, j: (d, g, tci(d, j), 0)),
            pl.BlockSpec((None, S5_SUB, 2 * S5_NS), lambda d, g, j: (d, g, 0)),
        ],
        out_shape=[jax.ShapeDtypeStruct((2, nbatch, n, S5_W), F32),
                   jax.ShapeDtypeStruct((2, x0.shape[1], 2 * S5_NS), F32)],
        scratch_shapes=[pltpu.VMEM((tq * S5_SUB, 2 * S5_NS), F32),
                        pltpu.VMEM((S5_SUB, 2 * S5_NS), F32)],
        compiler_params=_cparams("arbitrary", "arbitrary", "arbitrary"),
        name=f"s5_scan_{n}",
    )(u, pin_b, pout_b, b_re, b_im, c_re, c_im, lam8, x0)


def _s5_glu_kernel(y_ref, u_ref, d_ref, w_ref, b_ref, prev_ref, o_ref):
    y = y_ref[0] + y_ref[1] + u_ref[...] * d_ref[...]
    zg = jax.nn.gelu(y)
    gate = jax.nn.sigmoid(_bdot(zg.astype(BF16), w_ref[...]) + b_ref[...])
    o_ref[...] = (zg * gate).astype(o_ref.dtype)


def s5_glu(y2, z_s5, d, w_glu, b_glu, prev, *, row0):
    tm = 512
    r = y2.shape[1]
    blk0 = row0 // tm
    return pl.pallas_call(
        _s5_glu_kernel,
        grid=(r // tm,),
        in_specs=[
            pl.BlockSpec((2, tm, S5_W), lambda i: (0, i, 0)),
            pl.BlockSpec((tm, S5_W), lambda i: (blk0 + i, 0)),
            pl.BlockSpec((1, S5_W), lambda i: (0, 0)),
            pl.BlockSpec((S5_W, S5_W), lambda i: (0, 0)),
            pl.BlockSpec((1, S5_W), lambda i: (0, 0)),
            pl.BlockSpec(memory_space=pl.ANY),
        ],
        out_specs=pl.BlockSpec((tm, S5_W), lambda i: (blk0 + i, 0)),
        out_shape=jax.ShapeDtypeStruct(prev.shape, BF16),
        input_output_aliases={5: 0},
        compiler_params=_cparams("arbitrary"),
        name="s5_glu",
    )(y2, z_s5, d, w_glu, b_glu, prev)


def _out_proj_kernel(yh_ref, ya_ref, ys_ref, w_ref, x_ref, g1_ref, n2_ref, sc_ref, sh_ref,
                     wrh_ref, wrl_ref, br_ref, ltri_ref, ustr_ref,
                     x1_ref, h2_ref, rt_ref, gt_ref, cnt_ref):
    acc = _bdot(yh_ref[...], w_ref[:HY_W, :])
    acc += _bdot(ya_ref[...], w_ref[HY_W:HY_W + MLA_W, :])
    acc += _bdot(ys_ref[...], w_ref[HY_W + MLA_W:, :])
    x1 = x_ref[...] + g1_ref[...] * acc
    x1_ref[...] = x1
    h2 = _rms(x1, n2_ref[...]) * (1.0 + sc_ref[...]) + sh_ref[...]
    h_hi = h2.astype(BF16)
    h2_ref[...] = h_hi

    h_lo = (h2 - h_hi.astype(F32)).astype(BF16)
    logits_all = (_bdot(h_hi, wrh_ref[...]) + (_bdot(h_hi, wrl_ref[...]) + _bdot(h_lo, wrh_ref[...]))
                  + br_ref[...])
    neg = jnp.float32(-jnp.inf)
    nsub = x_ref.shape[0] // ROUTE_ROWS
    for sb in range(nsub):
        logits = logits_all[sb * ROUTE_ROWS:(sb + 1) * ROUTE_ROWS]
        lane = lax.broadcasted_iota(I32, logits.shape, 1)
        logits = jnp.where(lane < N_EXPERTS, logits, neg)
        idxs, exps = [], []
        v0 = None
        for k in range(TOP_K):
            m = jnp.max(logits, axis=-1, keepdims=True)
            idx = jnp.min(jnp.where(logits == m, lane, 128), axis=-1, keepdims=True)
            if k == 0:
                v0 = m
            idxs.append(idx)
            exps.append(jnp.exp(m - v0))
            logits = jnp.where(lane == idx, neg, logits)
        ohs = jnp.zeros(logits.shape, F32)
        for idx in idxs:
            ohs += (lane == idx).astype(F32)
        before = _bdot(ltri_ref[...], ohs.astype(BF16))
        cnt = jnp.sum(ohs, axis=0, keepdims=True)
        cnt8 = jnp.broadcast_to(cnt, (8, 128))
        seg = jnp.floor((cnt8 + (SEG_ALIGN - 1)) * (1.0 / SEG_ALIGN)) * SEG_ALIGN
        lower = _bdot(seg.astype(BF16), ustr_ref[...])[0:1]
        base = before + lower
        rt = jnp.zeros(logits.shape, I32)
        gt = jnp.zeros(logits.shape, F32)
        den = exps[0] + exps[1] + exps[2] + exps[3]
        for k in range(TOP_K):
            pos = jnp.sum(jnp.where(lane == idxs[k], base, 0.0), axis=-1, keepdims=True)
            rt = jnp.where(lane == k, idxs[k], rt)
            rt = jnp.where(lane == TOP_K + k, pos.astype(I32), rt)
            gt = jnp.where(lane == k, exps[k] / den, gt)
        rows = slice(sb * ROUTE_ROWS, (sb + 1) * ROUTE_ROWS)
        rt_ref[rows, :] = rt
        gt_ref[rows, :] = gt
        cnt_ref[sb] = cnt8.astype(I32)


def out_proj(y_hy, y_att, y_s5, w_out, x, g1, n2, sc2, sh2, w_r_hi, w_r_lo, b_r):
    tm = MM_ROWS
    t = x.shape[0]
    nsub = tm // ROUTE_ROWS
    row = lambda wd: pl.BlockSpec((tm, wd), lambda i: (i, 0))
    const = lambda a, b: pl.BlockSpec((a, b), lambda i: (0, 0))
    mod_spec = pl.BlockSpec((None, 1, D_MODEL), lambda i: (_mod_row(i, tm), 0, 0))
    r = np.arange(ROUTE_ROWS)
    ltri = jnp.asarray(r[None, :] < r[:, None], BF16)
    e = np.arange(128)
    ustr = jnp.asarray(e[:, None] < e[None, :], BF16)
    return pl.pallas_call(
        _out_proj_kernel,
        grid=(t // tm,),
        in_specs=[row(HY_W), row(MLA_W), row(S5_W), const(D_MODEL, D_MODEL), row(D_MODEL),
                  mod_spec, const(1, D_MODEL), mod_spec, mod_spec,
                  const(D_MODEL, 128), const(D_MODEL, 128), const(1, 128),
                  const(ROUTE_ROWS, ROUTE_ROWS), const(128, 128)],
        out_specs=[row(D_MODEL), row(D_MODEL), row(128), row(128),
                   pl.BlockSpec((nsub, 8, 128), lambda i: (i, 0, 0))],
        out_shape=[jax.ShapeDtypeStruct((t, D_MODEL), F32),
                   jax.ShapeDtypeStruct((t, D_MODEL), BF16),
                   jax.ShapeDtypeStruct((t, 128), I32),
                   jax.ShapeDtypeStruct((t, 128), F32),
                   jax.ShapeDtypeStruct((t // ROUTE_ROWS, 8, 128), I32)],
        compiler_params=_cparams("arbitrary"),
        name="out_proj",
    )(y_hy, y_att, y_s5, w_out, x, g1, n2, sc2, sh2, w_r_hi, w_r_lo, b_r, ltri, ustr)


def route_tables(cnt_arr):
    rb = EXPERT_ROWS
    cnt = cnt_arr[:, 0, :N_EXPERTS]
    cnt = (cnt + SEG_ALIGN - 1) // SEG_ALIGN * SEG_ALIGN
    counts = jnp.sum(cnt, axis=0)
    blk_cnt = (counts + rb - 1) // rb
    blk_end = jnp.cumsum(blk_cnt)
    blk_start = blk_end - blk_cnt
    total_blk = blk_end[-1]
    seg_start = blk_start[None, :] * rb + jnp.cumsum(cnt, axis=0) - cnt
    tail = jnp.stack([total_blk * rb, (N_BLK - total_blk) * (rb // SEG_MAX)])
    pad = jnp.concatenate([blk_start * rb + counts, blk_cnt * rb - counts, tail])
    seg = dict(start=seg_start.reshape(-1).astype(I32), cnt=cnt.reshape(-1).astype(I32), pad=pad.astype(I32))

    nj = D_MODEL // EXPERT_TN
    s = jnp.arange(N_BLK * nj, dtype=I32)
    blk0 = s // nj
    valid = blk0 < total_blk

    def expert_of(blk):
        return jnp.minimum(jnp.sum((blk_end[None, :] <= blk[:, None]).astype(I32), axis=1), N_EXPERTS - 1)

    last_e = expert_of((total_blk - 1)[None])[0]
    e = expert_of(blk0)
    c_e = jnp.maximum(blk_cnt[e], 1)
    local = s - blk_start[e] * nj
    j = local // c_e
    r = local % c_e
    steps = dict(
        x_blk=jnp.where(valid, blk_start[e] + r, 0),
        o_blk=jnp.where(valid, blk_start[e] + r, blk0),
        o_j=jnp.where(valid, j, s % nj),
        w_e=jnp.where(valid, e, last_e),
        w_j=jnp.where(valid, j, nj - 1),
        first=(valid & (r == 0)).astype(I32),
        rows=jnp.where(valid, jnp.clip(counts[e] - r * rb, 0, rb), 0).astype(I32),
    )
    return seg, steps


def _aligned(row):
    return row if isinstance(row, int) else pl.multiple_of(row, SEG_ALIGN)


def _segment_copies(cnt, fn):
    for b in range(SEG_ALIGN.bit_length() - 1, SEG_MAX.bit_length()):
        size = 1 << b
        start = (cnt >> (b + 1)) << (b + 1)

        @pl.when(((cnt >> b) & 1) == 1)
        def _(start=start, size=size):
            fn(start, size)


def _dispatch_kernel(start_ref, cnt_ref, pad_ref, rt_ref, h_ref, xs_ref, buf_ref, zero_ref, sem):
    b = pl.program_id(0)
    pos_t = jnp.transpose(rt_ref[...].astype(F32)).astype(I32)
    slot = lax.broadcasted_iota(I32, (SLOT_ROWS, ROUTE_ROWS), 0)
    sel = jnp.zeros((SLOT_ROWS, ROUTE_ROWS), F32)
    for k in range(TOP_K):
        sel += (slot == pos_t[TOP_K + k:TOP_K + k + 1, :]).astype(F32)
    xs = _bdot(sel.astype(BF16), h_ref[...])
    buf_ref[...] = _pack_bf16(xs)

    def copy(src_ref, src, dst, size):
        return pltpu.make_async_copy(src_ref.at[pl.ds(_aligned(src), size)],
                                     xs_ref.at[pl.ds(_aligned(dst), size)], sem)

    def for_segments(fn):
        def body(e, off):
            c = cnt_ref[b * N_EXPERTS + e]
            dst = start_ref[b * N_EXPERTS + e]
            _segment_copies(c, lambda st, size: fn(copy(buf_ref, off + st, dst + st, size)))
            return off + c
        lax.fori_loop(0, N_EXPERTS, body, 0)

    for_segments(lambda cp: cp.start())
    for_segments(lambda cp: cp.wait())

    @pl.when(b == pl.num_programs(0) - 1)
    def _():
        zero_ref[...] = jnp.zeros(zero_ref.shape, zero_ref.dtype)

        def for_pads(fn):
            def body(e, carry):
                dst = pad_ref[e]
                _segment_copies(pad_ref[N_EXPERTS + e], lambda st, size: fn(copy(zero_ref, 0, dst + st, size)))
                return carry
            lax.fori_loop(0, N_EXPERTS, body, 0)

        for_pads(lambda cp: cp.start())
        for_pads(lambda cp: cp.wait())

        tail0 = pad_ref[2 * N_EXPERTS]
        ntail = pad_ref[2 * N_EXPERTS + 1]

        def tail_copy(i):
            return copy(zero_ref, 0, tail0 + i * SEG_MAX, SEG_MAX)

        lax.fori_loop(0, ntail, lambda i, c: (tail_copy(i).start(), c)[1], 0)
        lax.fori_loop(0, ntail, lambda i, c: (tail_copy(i).wait(), c)[1], 0)


def dispatch(h2, rt, seg):
    nb = h2.shape[0] // ROUTE_ROWS
    return pl.pallas_call(
        _dispatch_kernel,
        grid_spec=pltpu.PrefetchScalarGridSpec(
            num_scalar_prefetch=3,
            grid=(nb,),
            in_specs=[pl.BlockSpec((ROUTE_ROWS, 128), lambda b, *_: (b, 0)),
                      pl.BlockSpec((ROUTE_ROWS, D_MODEL), lambda b, *_: (b, 0))],
            out_specs=pl.BlockSpec(memory_space=pl.ANY),
            scratch_shapes=[pltpu.VMEM((SLOT_ROWS, D_MODEL // 2), U32),
                            pltpu.VMEM((SEG_MAX, D_MODEL // 2), U32),
                            pltpu.SemaphoreType.DMA(())],
        ),
        out_shape=jax.ShapeDtypeStruct((N_SLOTS, D_MODEL // 2), U32),
        compiler_params=_cparams("arbitrary"),
        name="dispatch",
    )(seg["start"], seg["cnt"], seg["pad"], rt, h2)


def _pack_bf16(x):
    bits = lax.bitcast_convert_type(x.astype(BF16).astype(F32), U32)
    w = x.shape[1] // 2
    return (bits[:, w:] & jnp.uint32(0xFFFF0000)) | (bits[:, :w] >> 16)


def _unpack_bf16(words):
    lo = lax.bitcast_convert_type(words << 16, F32).astype(BF16)
    hi = lax.bitcast_convert_type(words & jnp.uint32(0xFFFF0000), F32).astype(BF16)
    return lo, hi


def _for_row_counts(rows, o_ref, fn):
    for nc in range(1, EXPERT_ROWS // EXPERT_CHUNK + 1):
        n = nc * EXPERT_CHUNK

        @pl.when((rows + EXPERT_CHUNK - 1) // EXPERT_CHUNK == nc)
        def _(n=n):
            o_ref[:n, :] = fn(n).astype(o_ref.dtype)
            if n < EXPERT_ROWS:
                o_ref[n:, :] = jnp.zeros((EXPERT_ROWS - n, o_ref.shape[1]), o_ref.dtype)

    @pl.when(rows == 0)
    def _():
        o_ref[...] = jnp.zeros(o_ref.shape, o_ref.dtype)


def _expert_up_kernel(xb_ref, ob_ref, oj_ref, we_ref, wj_ref, first_ref, rows_ref,
                      x_ref, wg_ref, wu_ref, bg_ref, bu_ref, o_ref, wgb_ref, wub_ref):
    s = pl.program_id(0)

    @pl.when(first_ref[s] == 1)
    def _():
        wgb_ref[...] = wg_ref[...].astype(BF16)
        wub_ref[...] = wu_ref[...].astype(BF16)

    def act(n):
        lo, hi = _unpack_bf16(x_ref[:n, :])
        half = D_MODEL // 2
        g = _bdot(lo, wgb_ref[:half, :]) + _bdot(hi, wgb_ref[half:, :]) + bg_ref[...]
        u = _bdot(lo, wub_ref[:half, :]) + _bdot(hi, wub_ref[half:, :]) + bu_ref[...]
        g = jnp.minimum(g, SWIGLU_LIMIT)
        u = jnp.clip(u, -SWIGLU_LIMIT, SWIGLU_LIMIT)
        return g * jax.nn.sigmoid(SWIGLU_ALPHA * g) * (u + 1.0)

    _for_row_counts(rows_ref[s], o_ref, act)


def expert_up(layer, xs, w_gu, b_gu, steps):
    tn = EXPERT_TN
    nj = D_EXPERT // tn
    n_steps = N_BLK * nj
    wspec = lambda up: pl.BlockSpec(
        (None, None, D_MODEL, tn), lambda s, xb, ob, oj, we, wj, fi, ro: (layer, we[s], 0, up * nj + wj[s]))
    bspec = lambda up: pl.BlockSpec(
        (None, None, 1, tn), lambda s, xb, ob, oj, we, wj, fi, ro: (layer, we[s], 0, up * nj + wj[s]))
    return pl.pallas_call(
        _expert_up_kernel,
        grid_spec=pltpu.PrefetchScalarGridSpec(
            num_scalar_prefetch=7,
            grid=(n_steps,),
            in_specs=[
                pl.BlockSpec((EXPERT_ROWS, D_MODEL // 2), lambda s, xb, ob, oj, we, wj, fi, ro: (xb[s], 0)),
                wspec(0), wspec(1), bspec(0), bspec(1),
            ],
            out_specs=pl.BlockSpec((EXPERT_ROWS, tn), lambda s, xb, ob, oj, we, wj, fi, ro: (ob[s], oj[s])),
            scratch_shapes=[pltpu.VMEM((D_MODEL, tn), BF16), pltpu.VMEM((D_MODEL, tn), BF16)],
        ),
        out_shape=jax.ShapeDtypeStruct((N_SLOTS, D_EXPERT), BF16),
        compiler_params=_cparams("arbitrary"),
        name="expert_up",
    )(steps["x_blk"], steps["o_blk"], steps["o_j"], steps["w_e"], steps["w_j"], steps["first"],
      steps["rows"], xs, w_gu, w_gu, b_gu, b_gu)


def _expert_down_kernel(xb_ref, ob_ref, oj_ref, we_ref, wj_ref, first_ref, rows_ref,
                        a_ref, w_ref, b_ref, o_ref, wb_ref):
    s = pl.program_id(0)

    @pl.when(first_ref[s] == 1)
    def _():
        wb_ref[...] = w_ref[...].astype(BF16)

    def down(n):
        y = _bdot(a_ref[:n, :], wb_ref[...]) + b_ref[...]
        return _pack_bf16(y)

    _for_row_counts(rows_ref[s], o_ref, down)


def expert_down(layer, act, w_dn, b_dn, steps):
    tn = EXPERT_TN
    nj = D_MODEL // tn
    n_steps = N_BLK * nj
    return pl.pallas_call(
        _expert_down_kernel,
        grid_spec=pltpu.PrefetchScalarGridSpec(
            num_scalar_prefetch=7,
            grid=(n_steps,),
            in_specs=[
                pl.BlockSpec((EXPERT_ROWS, D_EXPERT), lambda s, xb, ob, oj, we, wj, fi, ro: (xb[s], 0)),
                pl.BlockSpec((None, None, D_EXPERT, tn),
                             lambda s, xb, ob, oj, we, wj, fi, ro: (layer, we[s], 0, wj[s])),
                pl.BlockSpec((None, None, 1, tn),
                             lambda s, xb, ob, oj, we, wj, fi, ro: (layer, we[s], 0, wj[s])),
            ],
            out_specs=pl.BlockSpec((EXPERT_ROWS, tn // 2),
                                   lambda s, xb, ob, oj, we, wj, fi, ro: (ob[s], oj[s])),
            scratch_shapes=[pltpu.VMEM((D_EXPERT, tn), BF16)],
        ),
        out_shape=jax.ShapeDtypeStruct((N_SLOTS, D_MODEL // 2), U32),
        compiler_params=_cparams("arbitrary"),
        name="expert_down",
    )(steps["x_blk"], steps["o_blk"], steps["o_j"], steps["w_e"], steps["w_j"], steps["first"],
      steps["rows"], act, w_dn, b_dn)


def _combine_kernel(final, start_ref, cnt_ref, rt_ref, gt_ref, x_ref, g2_ref, nf_ref, ys_ref,
                    o_ref, buf_ref, sem):
    b = pl.program_id(0)

    def for_segments(fn):
        def body(e, off):
            c = cnt_ref[b * N_EXPERTS + e]
            src = start_ref[b * N_EXPERTS + e]
            _segment_copies(c, lambda st, size: fn(pltpu.make_async_copy(
                ys_ref.at[pl.ds(_aligned(src + st), size)], buf_ref.at[pl.ds(_aligned(off + st), size)], sem)))
            return off + c
        return lax.fori_loop(0, N_EXPERTS, body, 0)

    tail = ROUTE_ROWS * TOP_K
    buf_ref[tail:, :] = jnp.zeros((SLOT_ROWS - tail, buf_ref.shape[1]), U32)
    for_segments(lambda cp: cp.start())
    for_segments(lambda cp: cp.wait())

    rt = rt_ref[...]
    gt = gt_ref[...]
    slot = lax.broadcasted_iota(I32, (ROUTE_ROWS, SLOT_ROWS), 1)
    w = jnp.zeros((ROUTE_ROWS, SLOT_ROWS), F32)
    for k in range(TOP_K):
        w += jnp.where(slot == rt[:, TOP_K + k:TOP_K + k + 1], gt[:, k:k + 1], 0.0)
    w_hi = w.astype(BF16)
    w_lo = (w - w_hi.astype(F32)).astype(BF16)
    y_lo, y_hi = _unpack_bf16(buf_ref[...])
    a_lo = _bdot(w_hi, y_lo) + _bdot(w_lo, y_lo)
    a_hi = _bdot(w_hi, y_hi) + _bdot(w_lo, y_hi)
    hw = EXPERT_TN // 2
    acc = jnp.concatenate(
        [part[:, j * hw:(j + 1) * hw] for j in range(D_MODEL // EXPERT_TN) for part in (a_lo, a_hi)], axis=1)
    x2 = x_ref[...] + g2_ref[...] * acc
    if final:
        x2 = _rms(x2, nf_ref[...])
    o_ref[...] = x2


def combine(ys, rt, gates, seg, x1, g2, norm_f, final):
    tb = ROUTE_ROWS
    t = x1.shape[0]
    return pl.pallas_call(
        functools.partial(_combine_kernel, final),
        grid_spec=pltpu.PrefetchScalarGridSpec(
            num_scalar_prefetch=2,
            grid=(t // tb,),
            in_specs=[
                pl.BlockSpec((tb, 128), lambda i, *_: (i, 0)),
                pl.BlockSpec((tb, 128), lambda i, *_: (i, 0)),
                pl.BlockSpec((tb, D_MODEL), lambda i, *_: (i, 0)),
                pl.BlockSpec((None, 1, D_MODEL), lambda i, *_: (_mod_row(i, tb), 0, 0)),
                pl.BlockSpec((1, D_MODEL), lambda i, *_: (0, 0)),
                pl.BlockSpec(memory_space=pl.ANY),
            ],
            out_specs=pl.BlockSpec((tb, D_MODEL), lambda i, *_: (i, 0)),
            scratch_shapes=[pltpu.VMEM((SLOT_ROWS, D_MODEL // 2), U32), pltpu.SemaphoreType.DMA(())],
        ),
        out_shape=jax.ShapeDtypeStruct((t, D_MODEL), F32),
        compiler_params=_cparams("arbitrary"),
        name="combine",
    )(seg["start"], seg["cnt"], rt, gates, x1, g2, norm_f, ys)


def _rope_tables():
    rows = DEC_SEQ // GRID_W
    row = jnp.repeat(jnp.arange(rows, dtype=F32), GRID_W)
    col = jnp.tile(jnp.arange(GRID_W, dtype=F32), rows)
    half = QK_ROPE // 2
    inv = 1.0 / (ROPE_THETA ** (jnp.arange(0, half, 2, dtype=F32) / half))
    ar = row[:, None] * inv
    ac = col[:, None] * inv
    ang = jnp.concatenate([ar, ar, ac, ac], axis=-1)
    cos = jnp.concatenate([jnp.ones((SEQ, QK_ROPE), F32), jnp.cos(ang)], axis=0)
    sin = jnp.concatenate([jnp.zeros((SEQ, QK_ROPE), F32), jnp.sin(ang)], axis=0)
    scale = (QK_NOPE + QK_ROPE) ** -0.5
    tab_q = jnp.concatenate([jnp.ones((TAB_ROWS, QK_NOPE), F32), cos, sin], axis=-1) * scale
    tab_k = jnp.concatenate([cos, sin], axis=-1)
    return tab_q, tab_k


def _rotate_cols(w):
    r1, r2, c1, c2 = jnp.split(w, 4, axis=-1)
    return jnp.concatenate([-r2, r1, -c2, c1], axis=-1)


def _layer_weights(w_in, w_uq, w_uk, w_uv, w_out, s5_w_glu, w_router, b_router):
    hy, q, kv, kr, s5 = jnp.split(w_in, [HY_PROJ, HY_PROJ + Q_RANK, HY_PROJ + Q_RANK + KV_RANK,
                                         HY_PROJ + Q_RANK + KV_RANK + QK_ROPE], axis=-1)
    w_in_aug = jnp.concatenate([hy, q, kv, kr, _rotate_cols(kr), s5], axis=-1).astype(BF16)
    wq = w_uq.reshape(Q_RANK, MLA_HEADS, QK_NOPE + QK_ROPE)
    wq_rope = wq[..., QK_NOPE:]
    w_q = jnp.concatenate([wq[..., :QK_NOPE], wq_rope, _rotate_cols(wq_rope)], axis=-1)
    w_q = w_q.reshape(Q_RANK, MLA_HEADS * QH).astype(BF16)
    wk = w_uk.reshape(KV_RANK, MLA_HEADS, QK_NOPE)
    wk = jnp.concatenate([wk, jnp.zeros((KV_RANK, MLA_HEADS, 2 * QK_ROPE), F32)], axis=-1)
    eye = jnp.eye(QK_ROPE, dtype=F32)
    ek = jnp.concatenate([jnp.zeros((QK_ROPE, QK_NOPE), F32), eye, eye], axis=-1)
    ek = jnp.broadcast_to(ek[:, None, :], (QK_ROPE, MLA_HEADS, QH))
    wk = jnp.concatenate([wk, ek, jnp.zeros((QK_ROPE, MLA_HEADS, QH), F32)], axis=0)
    wv = jnp.concatenate([w_uv, jnp.zeros((2 * QK_ROPE, MLA_W), F32)], axis=0)
    w_kv = jnp.concatenate([wk.reshape(KVR_W, MLA_HEADS * QH), wv], axis=-1).astype(BF16)
    w_r = jnp.concatenate([w_router, jnp.zeros((D_MODEL, 128 - N_EXPERTS), F32)], axis=-1)
    b_r = jnp.concatenate([b_router, jnp.zeros((128 - N_EXPERTS,), F32)])[None, :]
    w_r_hi, w_r_lo = _split_bf16(w_r)
    return w_in_aug, w_q, w_kv, w_out.astype(BF16), s5_w_glu.astype(BF16), w_r_hi, w_r_lo, b_r


def _s5_block_diag(lbr, lbi, bbr, bbi, c_re, c_im):
    eye = jnp.eye(S5_GROUPS, dtype=F32)

    def bd_in(b):
        return jnp.einsum("dgpn,gh->dgnhp", b, eye).reshape(2, S5_W, S5_NS).astype(BF16)

    def bd_out(c):
        return jnp.einsum("dgnp,gh->dgphn", c, eye).reshape(2, S5_NS, S5_W).astype(BF16)

    lam = jnp.concatenate([lbr.reshape(2, 1, S5_NS), lbi.reshape(2, 1, S5_NS)], axis=-1)
    lam8 = jnp.broadcast_to(lam, (2, S5_SUB, 2 * S5_NS))
    return bd_in(bbr), bd_in(bbi), bd_out(c_re), bd_out(c_im), lam8


def kernel(x_prompt, x_sample, cache_ckv, cache_krope, state_ssm, c, c_ctx, w_ada, b_ada, norm1, norm2, w_in, w_out, hy_conv_w, hy_conv_b, hy_fw1, hy_fb1, hy_fw2, hy_fb2, hy_freq, hy_fw3, hy_decay, hy_skip, q_norm, kv_norm, w_uq, w_uk, w_uv, s5_lam_re, s5_lam_im, s5_log_dt, s5_b_re, s5_b_im, s5_c_re, s5_c_im, s5_d, s5_w_glu, s5_b_glu, w_router, b_router, w_gate_up, b_gate_up, w_down, b_down, norm_f):
    x = jnp.concatenate([x_prompt.reshape(T_P, D_MODEL), x_sample.reshape(T_S, D_MODEL)], axis=0)

    cond8 = jnp.concatenate([c_ctx[None, :], c, jnp.zeros((N_MOD - 1 - DEC_BATCH, D_MODEL), F32)], axis=0)
    mod = ada_modulation(cond8, w_ada, b_ada).reshape(DEPTH, N_MOD, 6, 1, D_MODEL)

    tab_q, tab_k = _rope_tables()
    dft = {}
    for n in (SEQ, DEC_SEQ):
        fwd, inv = _dft_tables(n)
        f_hi, f_lo = _split_bf16(fwd)
        dft[n] = (f_hi, f_lo, jnp.asarray(inv, F32).astype(BF16))
    fw1p = jnp.concatenate([hy_fw1, jnp.zeros((DEPTH, FILT_HID - POS_EMB, FILT_HID), F32)], axis=1)
    spectra = {n: hyena_spectra(n, fw1p, hy_fb1[:, None, :], hy_fw2, hy_fb2[:, None, :], hy_freq,
                                hy_fw3, hy_decay[:, None, :], dft[n][0], dft[n][1])
               for n in (SEQ, DEC_SEQ)}

    lbr, lbi, bbr, bbi = s5_params(s5_lam_re, s5_lam_im, s5_log_dt, s5_b_re, s5_b_im)
    b_gu4 = b_gate_up.reshape(DEPTH, N_EXPERTS, 1, 2 * D_EXPERT)
    b_dn4 = b_down.reshape(DEPTH, N_EXPERTS, 1, D_MODEL)
    bp_s = S5_SUB

    ckv_out, kr_out, ssm_out = [], [], []
    for l in range(DEPTH):
        w_in_aug, w_q, w_kv, w_out_b, w_glu_b, w_r_hi, w_r_lo, b_r = _layer_weights(
            w_in[l], w_uq[l], w_uk[l], w_uv[l], w_out[l], s5_w_glu[l], w_router[l], b_router[l])
        m = lambda k: mod[l, :, k]
        z_hy, z_q, z_kvr, z_s5 = in_proj(x, norm1[l][None, :], m(1), m(0), w_in_aug)

        cw, cb, sk = hy_conv_w[l], hy_conv_b[l][None, :], hy_skip[l]
        y_hy = hyena(z_hy, cw, cb, dft[SEQ][0], dft[SEQ][2], spectra[SEQ][l], sk,
                     n=SEQ, nbatch=BATCH, row0=0, nseq=8, t_out=T_ALL,
                     prev=jnp.zeros((T_ALL, HY_W), BF16))
        y_hy = hyena(z_hy, cw, cb, dft[DEC_SEQ][0], dft[DEC_SEQ][2], spectra[DEC_SEQ][l], sk,
                     n=DEC_SEQ, nbatch=DEC_BATCH, row0=T_P, nseq=1, t_out=T_ALL, prev=y_hy)

        q = q_proj(z_q, q_norm[l][None, :], w_q, tab_q)
        a = kv_prep(z_kvr, kv_norm[l][None, :], tab_k)
        a_p = a[:T_P]
        ckv_out.append(a_p[:, :KV_RANK].reshape(BATCH, SEQ, KV_RANK))
        kr_out.append(a_p[:, KV_RANK:KV_RANK + QK_ROPE].reshape(BATCH, SEQ, QK_ROPE))
        ctx = jnp.concatenate([cache_ckv[:, l], cache_krope[:, l],
                               jnp.zeros((DEC_BATCH, PAST_LEN, QK_ROPE), F32)], axis=-1)
        a_s = jnp.concatenate([a[T_P:].reshape(DEC_BATCH, DEC_SEQ, KVR_W), ctx], axis=1)
        lk_s = DEC_SEQ + PAST_LEN
        k_p, v_p = kv_up(a_p, w_kv)
        k_s, v_s = kv_up(a_s.reshape(DEC_BATCH * lk_s, KVR_W), w_kv)
        y_att = attention(q, k_p, v_p, nbatch=BATCH, lq=SEQ, lk=SEQ, row0=0, t_out=T_ALL,
                          prev=jnp.zeros((T_ALL, MLA_W), BF16))
        y_att = attention(q, k_s, v_s, nbatch=DEC_BATCH, lq=DEC_SEQ, lk=lk_s, row0=T_P, t_out=T_ALL,
                          prev=y_att)

        b_re, b_im, c_re, c_im, lam8 = _s5_block_diag(lbr[l], lbi[l], bbr[l], bbi[l],
                                                      s5_c_re[l], s5_c_im[l])
        st = state_ssm[:, l].reshape(DEC_BATCH, 2, 2 * S5_NS).transpose(1, 0, 2)
        x0_s = jnp.concatenate([st, jnp.zeros((2, bp_s - DEC_BATCH, 2 * S5_NS), F32)], axis=1)
        x0_p = jnp.zeros((2, BATCH, 2 * S5_NS), F32)
        y2_p, xf_p = s5_scan(z_s5, b_re, b_im, c_re, c_im, lam8, x0_p, nbatch=BATCH, n=SEQ, row0=0)
        y2_s, _ = s5_scan(z_s5, b_re, b_im, c_re, c_im, lam8, x0_s, nbatch=DEC_BATCH, n=DEC_SEQ, row0=T_P)
        ssm_out.append(xf_p.reshape(2, BATCH, 2, S5_GROUPS, S5_STATE).transpose(1, 0, 2, 3, 4))
        d_row, bg_row = s5_d[l][None, :], s5_b_glu[l][None, :]
        y_s5 = s5_glu(y2_p.reshape(2, T_P, S5_W), z_s5, d_row, w_glu_b, bg_row,
                      jnp.zeros((T_ALL, S5_W), BF16), row0=0)
        y_s5 = s5_glu(y2_s.reshape(2, T_S, S5_W), z_s5, d_row, w_glu_b, bg_row, y_s5, row0=T_P)

        x1, h2, rt, gates, cnt = out_proj(y_hy, y_att, y_s5, w_out_b, x, m(2), norm2[l][None, :],
                                          m(4), m(3), w_r_hi, w_r_lo, b_r)

        seg, steps = route_tables(cnt)
        xs = dispatch(h2, rt, seg)
        act = expert_up(l, xs, w_gate_up, b_gu4, steps)
        ys = expert_down(l, act, w_down, b_dn4, steps)
        x = combine(ys, rt, gates, seg, x1, m(5), norm_f[None, :], final=(l == DEPTH - 1))

    y_prompt = x[:T_P].reshape(BATCH, SEQ, D_MODEL)
    y_sample = x[T_P:].reshape(DEC_BATCH, DEC_SEQ, D_MODEL)
    return (y_prompt, y_sample, jnp.stack(ckv_out, axis=1), jnp.stack(kr_out, axis=1),
            jnp.stack(ssm_out, axis=1))
```

```python
import functools
import math

import numpy as np
import jax
import jax.numpy as jnp
from jax import lax
from jax.experimental import pallas as pl
from jax.experimental.pallas import tpu as pltpu

F32 = jnp.float32
BF16 = jnp.bfloat16
I32 = jnp.int32
U32 = jnp.uint32
HIGHEST = lax.Precision.HIGHEST

D_MODEL = 2048
BATCH = 32
SEQ = 256
DEPTH = 4
DEC_BATCH = 4
DEC_SEQ = 1024
PAST_LEN = 512
GRID_W = 64
EPS = 1e-6
HY_W = 512
HY_PROJ = 3 * HY_W
POS_BANDS = 16
POS_EMB = 2 * POS_BANDS + 1
FILT_HID = 64
MLA_HEADS = 8
Q_RANK = 512
KV_RANK = 256
QK_NOPE = 128
QK_ROPE = 64
V_DIM = 128
MLA_W = MLA_HEADS * V_DIM
ROPE_THETA = 10000.0
S5_W = 512
S5_GROUP_CH = 16
S5_GROUPS = 32
S5_STATE = 64
S5_NS = S5_GROUPS * S5_STATE
N_EXPERTS = 32
TOP_K = 4
D_EXPERT = D_MODEL
SWIGLU_LIMIT = 7.0
SWIGLU_ALPHA = 1.702

T_P = BATCH * SEQ
T_S = DEC_BATCH * DEC_SEQ
T_ALL = T_P + T_S
N_MOD = 8
QH = 256
KVR_W = KV_RANK + 2 * QK_ROPE
ROW_TILE = 256
TAB_ROWS = SEQ + DEC_SEQ

VMEM_LIMIT_BYTES = 56 * 1024 * 1024

EXPERT_ROWS = 512
N_ASSIGN = T_ALL * TOP_K
ROUTE_ROWS = 256
N_RBLK = T_ALL // ROUTE_ROWS
SEG_ALIGN = 8
SEG_MAX = ROUTE_ROWS
SLOT_ROWS = ROUTE_ROWS * TOP_K + N_EXPERTS * SEG_ALIGN
N_BLK = (N_ASSIGN + N_RBLK * N_EXPERTS * (SEG_ALIGN - 1)) // EXPERT_ROWS + 1 + N_EXPERTS
N_SLOTS = N_BLK * EXPERT_ROWS
EXPERT_TN = 1024
EXPERT_CHUNK = 128
MM_ROWS = 512


def _cparams(*sem):
    return pltpu.CompilerParams(dimension_semantics=sem, vmem_limit_bytes=VMEM_LIMIT_BYTES)


def _mod_row(i, tm):
    npb = T_P // tm
    sb = DEC_SEQ // tm
    return jnp.where(i < npb, 0, 1 + (i - npb) // sb)


def _tab_blk(i, tm):
    npb = T_P // tm
    pb = SEQ // tm
    sb = DEC_SEQ // tm
    return jnp.where(i < npb, i % pb, pb + (i - npb) % sb)


def _rms(x, g):
    return x * lax.rsqrt(jnp.mean(x * x, axis=-1, keepdims=True) + EPS) * g


def _bdot(a, b):
    return jnp.dot(a, b, preferred_element_type=F32)


def _ada_kernel(c_ref, w_ref, b_ref, o_ref):
    s = jax.nn.silu(c_ref[...]).astype(BF16)
    o_ref[...] = _bdot(s, w_ref[...].astype(BF16)) + b_ref[...]


def ada_modulation(cond8, w_ada, b_ada):
    tn = 1024
    n = 6 * D_MODEL
    return pl.pallas_call(
        _ada_kernel,
        grid=(DEPTH, n // tn),
        in_specs=[
            pl.BlockSpec((N_MOD, D_MODEL), lambda l, j: (0, 0)),
            pl.BlockSpec((None, D_MODEL, tn), lambda l, j: (l, 0, j)),
            pl.BlockSpec((None, 1, tn), lambda l, j: (l, 0, j)),
        ],
        out_specs=pl.BlockSpec((None, N_MOD, tn), lambda l, j: (l, 0, j)),
        out_shape=jax.ShapeDtypeStruct((DEPTH, N_MOD, n), F32),
        compiler_params=_cparams("arbitrary", "arbitrary"),
        name="ada_modulation",
    )(cond8, w_ada, b_ada.reshape(DEPTH, 1, n))


IN_COLS = (HY_PROJ, Q_RANK, KVR_W, S5_W)
IN_W_AUG = sum(IN_COLS)


def _in_proj_kernel(x_ref, g_ref, sc_ref, sh_ref, w_ref, zhy_ref, zq_ref, zkvr_ref, zs5_ref):
    h = _rms(x_ref[...], g_ref[...]) * (1.0 + sc_ref[...]) + sh_ref[...]
    h = h.astype(BF16)
    off = 0
    for o_ref, wd in zip((zhy_ref, zq_ref, zkvr_ref, zs5_ref), IN_COLS):
        o_ref[...] = _bdot(h, w_ref[:, off:off + wd])
        off += wd


def in_proj(x, g, sc, sh, w_aug):
    tm = MM_ROWS
    t = x.shape[0]
    mod_spec = pl.BlockSpec((None, 1, D_MODEL), lambda i: (_mod_row(i, tm), 0, 0))
    return pl.pallas_call(
        _in_proj_kernel,
        grid=(t // tm,),
        in_specs=[
            pl.BlockSpec((tm, D_MODEL), lambda i: (i, 0)),
            pl.BlockSpec((1, D_MODEL), lambda i: (0, 0)),
            mod_spec, mod_spec,
            pl.BlockSpec((D_MODEL, IN_W_AUG), lambda i: (0, 0)),
        ],
        out_specs=[pl.BlockSpec((tm, wd), lambda i: (i, 0)) for wd in IN_COLS],
        out_shape=[jax.ShapeDtypeStruct((t, wd), F32) for wd in IN_COLS],
        compiler_params=_cparams("arbitrary"),
        name="in_proj",
    )(x, g, sc, sh, w_aug)


def _dft_tables(n):
    f = np.arange(n)[:, None]
    s = np.arange(n)[None, :]
    ang = (np.pi / n) * ((f * s) % (2 * n)).astype(np.float64)
    cos = np.cos(ang)
    sin = np.sin(ang)
    nyq = np.where(np.arange(n) % 2 == 0, 1.0, -1.0)
    fs = sin.copy()
    fs[0, :] = nyq
    fwd = np.concatenate([cos, fs], axis=0)
    wgt = np.full((n,), 2.0)
    wgt[0] = 1.0
    gc = (cos * wgt[:, None]).T / (2 * n)
    gs = (sin * wgt[:, None]).T / (2 * n)
    gs[:, 0] = nyq / (2 * n)
    inv = np.concatenate([gc, gs], axis=1)
    return fwd, inv


def _split_bf16(a):
    hi = jnp.asarray(a, F32).astype(BF16)
    lo = (jnp.asarray(a, F32) - hi.astype(F32)).astype(BF16)
    return hi, lo


def _hyena_spec_kernel(n, feat_ref, t_ref, fw1_ref, fb1_ref, fw2_ref, fb2_ref, freq_ref,
                       w3f0_ref, w3f1_ref, w3b0_ref, w3b1_ref,
                       dcf0_ref, dcf1_ref, dcb0_ref, dcb1_ref,
                       fhi_ref, flo_ref, o_ref):
    hdot = functools.partial(jnp.dot, precision=HIGHEST, preferred_element_type=F32)
    h = jnp.sin(freq_ref[0:1, :] * (hdot(feat_ref[...], fw1_ref[...]) + fb1_ref[...]))
    h = jnp.sin(freq_ref[1:2, :] * (hdot(h, fw2_ref[...]) + fb2_ref[...]))
    tcol = t_ref[...]
    row0 = lax.broadcasted_iota(I32, (n, 1), 0) == 0

    def filt(w3_ref, dc_ref):
        return hdot(h, w3_ref[...]) * jnp.exp(-tcol * jnp.abs(dc_ref[...]))

    def dft(fpart_hi, fpart_lo, k):
        k_hi = k.astype(BF16)
        k_lo = (k - k_hi.astype(F32)).astype(BF16)
        return _bdot(fpart_hi, k_hi) + (_bdot(fpart_hi, k_lo) + _bdot(fpart_lo, k_hi))

    for o, (wf, wb, df, db) in enumerate(((w3f0_ref, w3b0_ref, dcf0_ref, dcb0_ref),
                                          (w3f1_ref, w3b1_ref, dcf1_ref, dcb1_ref))):
        fwd = filt(wf, df)
        bwd = jnp.where(row0, 0.0, filt(wb, db))
        ks = dft(fhi_ref[...], flo_ref[...], fwd + bwd)
        kd = dft(fhi_ref[n:, :], flo_ref[n:, :], fwd - bwd)
        kr = ks[:n]
        ksp = jnp.where(row0, ks[n:n + 1], kd)
        o_ref[3 * o + 0] = kr
        o_ref[3 * o + 1] = jnp.where(row0, 0.0, ksp)
        o_ref[3 * o + 2] = jnp.where(row0, ksp, kr)


def hyena_spectra(n, fw1p, fb1, fw2, fb2, freq, fw3, decay, fwd_hi, fwd_lo):
    tc = 256
    nc = HY_W // tc
    f32 = F32
    t = jnp.linspace(0.0, 1.0, n, dtype=f32)[:, None]
    w = (2.0 * math.pi / n) * jnp.arange(n, dtype=f32)[:, None]
    bands = jnp.linspace(1e-4, POS_BANDS - 1, POS_BANDS, dtype=f32)
    feat = jnp.concatenate([t, jnp.cos(w * bands), -jnp.sin(w * bands),
                            jnp.zeros((n, FILT_HID - POS_EMB), f32)], axis=-1)

    def w3_spec(d, o):
        return pl.BlockSpec((None, FILT_HID, tc), lambda l, c: (l, 0, (d * 2 + o) * nc + c))

    def dc_spec(d, o):
        return pl.BlockSpec((None, 1, tc), lambda l, c: (l, 0, (d * 2 + o) * nc + c))

    const2 = lambda shape: pl.BlockSpec(shape, lambda l, c: (0, 0))
    lay3 = lambda a, b: pl.BlockSpec((None, a, b), lambda l, c: (l, 0, 0))
    return pl.pallas_call(
        functools.partial(_hyena_spec_kernel, n),
        grid=(DEPTH, nc),
        in_specs=[
            const2((n, FILT_HID)), const2((n, 1)),
            lay3(FILT_HID, FILT_HID), lay3(1, FILT_HID), lay3(FILT_HID, FILT_HID), lay3(1, FILT_HID),
            lay3(2, FILT_HID),
            w3_spec(0, 0), w3_spec(0, 1), w3_spec(1, 0), w3_spec(1, 1),
            dc_spec(0, 0), dc_spec(0, 1), dc_spec(1, 0), dc_spec(1, 1),
            const2((2 * n, n)), const2((2 * n, n)),
        ],
        out_specs=pl.BlockSpec((None, 6, n, tc), lambda l, c: (l, 0, 0, c)),
        out_shape=jax.ShapeDtypeStruct((DEPTH, 6, n, HY_W), F32),
        compiler_params=_cparams("arbitrary", "arbitrary"),
        name=f"hyena_spectra_{n}",
    )(feat, t, fw1p, fb1, fw2, fb2, freq, fw3, fw3, fw3, fw3, decay, decay, decay, decay,
      fwd_hi, fwd_lo)


def _hyena_kernel(n, nseq, zv_ref, z1_ref, z2_ref, wv_ref, w1_ref, w2_ref, bv_ref, b1_ref, b2_ref,
                  f_ref, g_ref, spec_ref, skip_ref, *rest):
    o_ref = rest[-1]
    row = lax.broadcasted_iota(I32, (n, 1), 0)
    first = row == 0
    last = row == n - 1

    def conv3(z, w_ref, b_ref):
        zm = jnp.where(first, 0.0, pltpu.roll(z, 1, 0))
        zp = jnp.where(last, 0.0, pltpu.roll(z, n - 1, 0))
        return zm * w_ref[0:1, :] + z * w_ref[1:2, :] + zp * w_ref[2:3, :] + b_ref[...]

    def long_conv(u, o):
        uu = _bdot(f_ref[...], u.astype(BF16))
        a = uu[:n]
        b = uu[n:]
        kr = spec_ref[3 * o + 0]
        ks = spec_ref[3 * o + 1]
        dd = spec_ref[3 * o + 2]
        yre = (kr * a - ks * b).astype(BF16)
        zz = (dd * b + ks * a).astype(BF16)
        y = _bdot(g_ref[:, :n], yre) + _bdot(g_ref[:, n:], zz)
        return y + u * skip_ref[o:o + 1, :]

    for s in range(nseq):
        rows = pl.ds(s * n, n)
        v = conv3(zv_ref[rows, :], wv_ref, bv_ref)
        x1 = conv3(z1_ref[rows, :], w1_ref, b1_ref)
        x2 = conv3(z2_ref[rows, :], w2_ref, b2_ref)
        y = x1 * long_conv(v, 0)
        y = x2 * long_conv(y, 1)
        o_ref[rows, :] = y.astype(o_ref.dtype)


def hyena(z_hy, conv_w, conv_b, f_hi, g_inv, spec, skip, *, n, nbatch, row0, nseq, t_out, prev=None):
    tc = 256
    nc = HY_W // tc
    rb = nseq * n
    blk0 = row0 // rb
    assert row0 % rb == 0 and nbatch % nseq == 0

    def zspec(part):
        return pl.BlockSpec((rb, tc), lambda c, b: (blk0 + b, part * nc + c))

    def wspec(part, rows):
        return pl.BlockSpec((rows, tc), lambda c, b: (0, part * nc + c))

    in_specs = [zspec(0), zspec(1), zspec(2), wspec(0, 3), wspec(1, 3), wspec(2, 3),
                wspec(0, 1), wspec(1, 1), wspec(2, 1),
                pl.BlockSpec((2 * n, n), lambda c, b: (0, 0)),
                pl.BlockSpec((n, 2 * n), lambda c, b: (0, 0)),
                pl.BlockSpec((6, n, tc), lambda c, b: (0, 0, c)),
                pl.BlockSpec((2, tc), lambda c, b: (0, c))]
    args = [z_hy, z_hy, z_hy, conv_w, conv_w, conv_w, conv_b, conv_b, conv_b, f_hi, g_inv, spec, skip]
    aliases = {}
    if prev is not None:
        in_specs.append(pl.BlockSpec(memory_space=pl.ANY))
        args.append(prev)
        aliases = {len(args) - 1: 0}
    return pl.pallas_call(
        functools.partial(_hyena_kernel, n, nseq),
        grid=(nc, nbatch // nseq),
        in_specs=in_specs,
        out_specs=pl.BlockSpec((rb, tc), lambda c, b: (blk0 + b, c)),
        out_shape=jax.ShapeDtypeStruct((t_out, HY_W), BF16),
        input_output_aliases=aliases,
        compiler_params=_cparams("arbitrary", "arbitrary"),
        name=f"hyena_{n}",
    )(*args)


def _q_proj_kernel(z_ref, g_ref, w_ref, tab_ref, o_ref):
    h = _rms(z_ref[...], g_ref[...]).astype(BF16)
    tab = tab_ref[...]
    for hd in range(MLA_HEADS):
        cols = slice(hd * QH, (hd + 1) * QH)
        o_ref[:, cols] = (_bdot(h, w_ref[:, cols]) * tab).astype(BF16)


def q_proj(z_q, g, w_q, tab_q):
    tm = ROW_TILE
    t = z_q.shape[0]
    return pl.pallas_call(
        _q_proj_kernel,
        grid=(t // tm,),
        in_specs=[
            pl.BlockSpec((tm, Q_RANK), lambda i: (i, 0)),
            pl.BlockSpec((1, Q_RANK), lambda i: (0, 0)),
            pl.BlockSpec((Q_RANK, MLA_HEADS * QH), lambda i: (0, 0)),
            pl.BlockSpec((tm, QH), lambda i: (_tab_blk(i, tm), 0)),
        ],
        out_specs=pl.BlockSpec((tm, MLA_HEADS * QH), lambda i: (i, 0)),
        out_shape=jax.ShapeDtypeStruct((t, MLA_HEADS * QH), BF16),
        compiler_params=_cparams("arbitrary"),
        name="q_proj",
    )(z_q, g, w_q, tab_q)


def _kv_prep_kernel(z_ref, g_ref, tab_ref, o_ref):
    o_ref[:, :KV_RANK] = _rms(z_ref[:, :KV_RANK], g_ref[...])
    t = z_ref[:, KV_RANK:] * tab_ref[...]
    o_ref[:, KV_RANK:] = t + pltpu.roll(t, QK_ROPE, 1)


def kv_prep(z_kvr, g, tab_k):
    tm = ROW_TILE
    t = z_kvr.shape[0]
    return pl.pallas_call(
        _kv_prep_kernel,
        grid=(t // tm,),
        in_specs=[
            pl.BlockSpec((tm, KVR_W), lambda i: (i, 0)),
            pl.BlockSpec((1, KV_RANK), lambda i: (0, 0)),
            pl.BlockSpec((tm, 2 * QK_ROPE), lambda i: (_tab_blk(i, tm), 0)),
        ],
        out_specs=pl.BlockSpec((tm, KVR_W), lambda i: (i, 0)),
        out_shape=jax.ShapeDtypeStruct((t, KVR_W), F32),
        compiler_params=_cparams("arbitrary"),
        name="kv_prep",
    )(z_kvr, g, tab_k)


def _kv_up_kernel(a_ref, w_ref, k_ref, v_ref):
    a = a_ref[...].astype(BF16)
    nk = MLA_HEADS * QH
    k_ref[...] = _bdot(a, w_ref[:, :nk]).astype(BF16)
    v_ref[...] = _bdot(a, w_ref[:, nk:]).astype(BF16)


def kv_up(a, w_kv):
    tm = 512
    t = a.shape[0]
    nk = MLA_HEADS * QH
    return pl.pallas_call(
        _kv_up_kernel,
        grid=(t // tm,),
        in_specs=[
            pl.BlockSpec((tm, KVR_W), lambda i: (i, 0)),
            pl.BlockSpec((KVR_W, nk + MLA_W), lambda i: (0, 0)),
        ],
        out_specs=[pl.BlockSpec((tm, nk), lambda i: (i, 0)),
                   pl.BlockSpec((tm, MLA_W), lambda i: (i, 0))],
        out_shape=[jax.ShapeDtypeStruct((t, nk), BF16), jax.ShapeDtypeStruct((t, MLA_W), BF16)],
        compiler_params=_cparams("arbitrary"),
        name="kv_up",
    )(a, w_kv)


def _attn_kernel(q_ref, k_ref, v_ref, *rest):
    o_ref = rest[-1]
    for hd in range(MLA_HEADS):
        q = q_ref[:, hd * QH:(hd + 1) * QH]
        k = k_ref[:, hd * QH:(hd + 1) * QH]
        s = lax.dot_general(q, k, (((1,), (1,)), ((), ())), preferred_element_type=F32)
        m = jnp.max(s, axis=-1, keepdims=True)
        p = jnp.exp(s - m)
        l = jnp.sum(p, axis=-1, keepdims=True)
        o = _bdot(p.astype(BF16), v_ref[:, hd * V_DIM:(hd + 1) * V_DIM])
        o_ref[:, hd * V_DIM:(hd + 1) * V_DIM] = (o / l).astype(o_ref.dtype)


def attention(q, k, v, *, nbatch, lq, lk, row0, t_out, prev=None):
    tq = ROW_TILE
    nq = lq // tq
    blk0 = row0 // tq
    in_specs = [
        pl.BlockSpec((tq, MLA_HEADS * QH), lambda b, i: (blk0 + b * nq + i, 0)),
        pl.BlockSpec((lk, MLA_HEADS * QH), lambda b, i: (b, 0)),
        pl.BlockSpec((lk, MLA_W), lambda b, i: (b, 0)),
    ]
    args = [q, k, v]
    aliases = {}
    if prev is not None:
        in_specs.append(pl.BlockSpec(memory_space=pl.ANY))
        args.append(prev)
        aliases = {3: 0}
    return pl.pallas_call(
        _attn_kernel,
        grid=(nbatch, nq),
        in_specs=in_specs,
        out_specs=pl.BlockSpec((tq, MLA_W), lambda b, i: (blk0 + b * nq + i, 0)),
        out_shape=jax.ShapeDtypeStruct((t_out, MLA_W), BF16),
        input_output_aliases=aliases,
        compiler_params=_cparams("arbitrary", "arbitrary"),
        name=f"attention_{lk}",
    )(*args)


def _s5_param_kernel(lr_ref, li_ref, dt_ref, lrx_ref, lix_ref, dtx_ref, bre_ref, bim_ref,
                     lbr_ref, lbi_ref, bbr_ref, bbi_ref):
    def lam_bar(lr, li, ldt):
        lr = jnp.minimum(lr, -1e-4)
        dt = jnp.exp(ldt)
        e = jnp.exp(lr * dt)
        return lr, e * jnp.cos(li * dt), e * jnp.sin(li * dt)

    _, lbr, lbi = lam_bar(lr_ref[...], li_ref[...], dt_ref[...])
    lbr_ref[...] = lbr
    lbi_ref[...] = lbi
    lr, xr, xi = lam_bar(lrx_ref[...], lix_ref[...], dtx_ref[...])
    li = lix_ref[...]
    nr = xr - 1.0
    den = lr * lr + li * li
    cr = (nr * lr + xi * li) / den
    ci = (xi * lr - nr * li) / den
    bbr_ref[...] = cr * bre_ref[...] - ci * bim_ref[...]
    bbi_ref[...] = cr * bim_ref[...] + ci * bre_ref[...]


def s5_params(lam_re, lam_im, log_dt, b_re, b_im):
    r = DEPTH * 2 * S5_GROUPS
    ldt = jnp.broadcast_to(log_dt[..., None], lam_re.shape)
    small = [a.reshape(r, S5_STATE) for a in (lam_re, lam_im, ldt)]
    wide = [jnp.repeat(a, S5_GROUP_CH, axis=-1) for a in small]
    bs = [a.reshape(r, S5_STATE * S5_GROUP_CH) for a in (b_re, b_im)]
    outs = pl.pallas_call(
        _s5_param_kernel,
        out_shape=[jax.ShapeDtypeStruct((r, S5_STATE), F32)] * 2
        + [jax.ShapeDtypeStruct((r, S5_STATE * S5_GROUP_CH), F32)] * 2,
        name="s5_params",
    )(*small, *wide, *bs)
    lbr, lbi, bbr, bbi = outs
    shp = (DEPTH, 2, S5_GROUPS, S5_STATE)
    return (lbr.reshape(shp), lbi.reshape(shp),
            bbr.reshape(shp + (S5_GROUP_CH,)), bbi.reshape(shp + (S5_GROUP_CH,)))


S5_SUB = 8
S5_LANES = 512
S5_CH = S5_LANES // S5_STATE * S5_GROUP_CH


def _s5_scan_kernel(tq, nseq, u_ref, pin_ref, pout_ref, bre_ref, bim_ref, cre_ref, cim_ref, lam_ref, x0_ref,
                    y_ref, xf_ref, s_ref, x_ref):
    d = pl.program_id(0)
    j = pl.program_id(2)

    @pl.when(j == 0)
    def _():
        x_ref[...] = x0_ref[...]

    u = u_ref[...].reshape(nseq * tq, S5_W).astype(BF16)
    u = _bdot(pin_ref[...], u).astype(BF16)
    for c in range(S5_NS // S5_LANES):
        ch = slice(c * S5_CH, (c + 1) * S5_CH)
        st = slice(c * S5_LANES, (c + 1) * S5_LANES)
        s_ref[:, st] = _bdot(u[:, ch], bre_ref[c])
        s_ref[:, S5_NS + c * S5_LANES:S5_NS + (c + 1) * S5_LANES] = _bdot(u[:, ch], bim_ref[c])

    for c in range(S5_NS // S5_LANES):
        re = pl.ds(c * S5_LANES, S5_LANES)
        im = pl.ds(S5_NS + c * S5_LANES, S5_LANES)
        lr = lam_ref[:, re]
        li = lam_ref[:, im]

        def step(i, carry):
            xr, xi = carry
            t = jnp.where(d == 0, i, tq - 1 - i)
            r = pl.ds(pl.multiple_of(t * S5_SUB, S5_SUB), S5_SUB)
            nr = lr * xr - li * xi + s_ref[r, re]
            ni = lr * xi + li * xr + s_ref[r, im]
            s_ref[r, re] = nr
            s_ref[r, im] = ni
            return nr, ni

        xr, xi = lax.fori_loop(0, tq, step, (x_ref[:, re], x_ref[:, im]))
        x_ref[:, re] = xr
        x_ref[:, im] = xi

    for c in range(S5_NS // S5_LANES):
        ch = slice(c * S5_CH, (c + 1) * S5_CH)
        st = slice(c * S5_LANES, (c + 1) * S5_LANES)
        y = (_bdot(s_ref[:, st].astype(BF16), cre_ref[c])
             - _bdot(s_ref[:, S5_NS + c * S5_LANES:S5_NS + (c + 1) * S5_LANES].astype(BF16), cim_ref[c]))
        y_hi = y.astype(BF16)
        y_lo = (y - y_hi.astype(F32)).astype(BF16)
        y = _bdot(pout_ref[...], y_hi) + _bdot(pout_ref[...], y_lo)
        y_ref[:, :, ch] = y.reshape(nseq, tq, S5_CH)

    @pl.when(j == pl.num_programs(2) - 1)
    def _():
        xf_ref[...] = x_ref[...]


def s5_scan(z_s5, b_re, b_im, c_re, c_im, lam8, x0, *, layer, nbatch, n, row0):
    tq = 64
    nch = n // tq
    nseq = min(nbatch, S5_SUB)
    u = z_s5.reshape(-1, n, S5_W)
    g0 = row0 // n // nseq
    assert row0 % (n * nseq) == 0 and nbatch % nseq == 0
    t_i, b_i = np.meshgrid(np.arange(tq), np.arange(nseq), indexing="ij")
    pin = np.zeros((tq * S5_SUB, nseq * tq), np.float32)
    pin[(t_i * S5_SUB + b_i).ravel(), (b_i * tq + t_i).ravel()] = 1.0
    pin_b = jnp.asarray(pin, BF16)
    pout_b = jnp.asarray(pin.T, BF16)

    def tci(d, j):
        return d * (nch - 1) + (1 - 2 * d) * j

    nslab = S5_NS // S5_LANES
    slabs = lambda a, b: pl.BlockSpec((None, None, nslab, a, b), lambda d, g, j: (layer, d, 0, 0, 0))
    const = lambda a, b: pl.BlockSpec((a, b), lambda d, g, j: (0, 0))
    return pl.pallas_call(
        functools.partial(_s5_scan_kernel, tq, nseq),
        grid=(2, nbatch // nseq, nch),
        in_specs=[
            pl.BlockSpec((nseq, tq, S5_W), lambda d, g, j: (g0 + g, tci(d, j), 0)),
            const(tq * S5_SUB, nseq * tq), const(nseq * tq, tq * S5_SUB),
            slabs(S5_CH, S5_LANES), slabs(S5_CH, S5_LANES), slabs(S5_LANES, S5_CH), slabs(S5_LANES, S5_CH),
            pl.BlockSpec((None, None, S5_SUB, 2 * S5_NS), lambda d, g, j: (layer, d, 0, 0)),
            pl.BlockSpec((None, S5_SUB, 2 * S5_NS), lambda d, g, j: (d, g, 0)),
        ],
        out_specs=[
            pl.BlockSpec((None, nseq, tq, S5_W), lambda d, g, j: (d, g, tci(d, j), 0)),
            pl.BlockSpec((None, S5_SUB, 2 * S5_NS), lambda d, g, j: (d, g, 0)),
        ],
        out_shape=[jax.ShapeDtypeStruct((2, nbatch, n, S5_W), F32),
                   jax.ShapeDtypeStruct((2, x0.shape[1], 2 * S5_NS), F32)],
        scratch_shapes=[pltpu.VMEM((tq * S5_SUB, 2 * S5_NS), F32),
                        pltpu.VMEM((S5_SUB, 2 * S5_NS), F32)],
        compiler_params=_cparams("arbitrary", "arbitrary", "arbitrary"),
        name=f"s5_scan_{n}",
    )(u, pin_b, pout_b, b_re, b_im, c_re, c_im, lam8, x0)


def _s5_glu_kernel(y_ref, u_ref, d_ref, w_ref, b_ref, prev_ref, o_ref):
    y = y_ref[0] + y_ref[1] + u_ref[...] * d_ref[...]
    zg = jax.nn.gelu(y)
    gate = jax.nn.sigmoid(_bdot(zg.astype(BF16), w_ref[...]) + b_ref[...])
    o_ref[...] = (zg * gate).astype(o_ref.dtype)


def s5_glu(y2, z_s5, d, w_glu, b_glu, prev, *, row0):
    tm = 512
    r = y2.shape[1]
    blk0 = row0 // tm
    return pl.pallas_call(
        _s5_glu_kernel,
        grid=(r // tm,),
        in_specs=[
            pl.BlockSpec((2, tm, S5_W), lambda i: (0, i, 0)),
            pl.BlockSpec((tm, S5_W), lambda i: (blk0 + i, 0)),
            pl.BlockSpec((1, S5_W), lambda i: (0, 0)),
            pl.BlockSpec((S5_W, S5_W), lambda i: (0, 0)),
            pl.BlockSpec((1, S5_W), lambda i: (0, 0)),
            pl.BlockSpec(memory_space=pl.ANY),
        ],
        out_specs=pl.BlockSpec((tm, S5_W), lambda i: (blk0 + i, 0)),
        out_shape=jax.ShapeDtypeStruct(prev.shape, BF16),
        input_output_aliases={5: 0},
        compiler_params=_cparams("arbitrary"),
        name="s5_glu",
    )(y2, z_s5, d, w_glu, b_glu, prev)


def _out_proj_kernel(yh_ref, ya_ref, ys_ref, w_ref, x_ref, g1_ref, n2_ref, sc_ref, sh_ref,
                     wrh_ref, wrl_ref, br_ref, ltri_ref, ustr_ref,
                     x1_ref, h2_ref, rt_ref, gt_ref, cnt_ref):
    acc = _bdot(yh_ref[...], w_ref[:HY_W, :])
    acc += _bdot(ya_ref[...], w_ref[HY_W:HY_W + MLA_W, :])
    acc += _bdot(ys_ref[...], w_ref[HY_W + MLA_W:, :])
    x1 = x_ref[...] + g1_ref[...] * acc
    x1_ref[...] = x1
    h2 = _rms(x1, n2_ref[...]) * (1.0 + sc_ref[...]) + sh_ref[...]
    h_hi = h2.astype(BF16)
    h2_ref[...] = h_hi

    h_lo = (h2 - h_hi.astype(F32)).astype(BF16)
    logits_all = (_bdot(h_hi, wrh_ref[...]) + (_bdot(h_hi, wrl_ref[...]) + _bdot(h_lo, wrh_ref[...]))
                  + br_ref[...])
    neg = jnp.float32(-jnp.inf)
    nsub = x_ref.shape[0] // ROUTE_ROWS
    for sb in range(nsub):
        logits = logits_all[sb * ROUTE_ROWS:(sb + 1) * ROUTE_ROWS]
        lane = lax.broadcasted_iota(I32, logits.shape, 1)
        logits = jnp.where(lane < N_EXPERTS, logits, neg)
        idxs, exps = [], []
        v0 = None
        for k in range(TOP_K):
            m = jnp.max(logits, axis=-1, keepdims=True)
            idx = jnp.min(jnp.where(logits == m, lane, 128), axis=-1, keepdims=True)
            if k == 0:
                v0 = m
            idxs.append(idx)
            exps.append(jnp.exp(m - v0))
            logits = jnp.where(lane == idx, neg, logits)
        ohs = jnp.zeros(logits.shape, F32)
        for idx in idxs:
            ohs += (lane == idx).astype(F32)
        before = _bdot(ltri_ref[...], ohs.astype(BF16))
        cnt = jnp.sum(ohs, axis=0, keepdims=True)
        cnt8 = jnp.broadcast_to(cnt, (8, 128))
        seg = jnp.floor((cnt8 + (SEG_ALIGN - 1)) * (1.0 / SEG_ALIGN)) * SEG_ALIGN
        lower = _bdot(seg.astype(BF16), ustr_ref[...])[0:1]
        base = before + lower
        rt = jnp.zeros(logits.shape, I32)
        gt = jnp.zeros(logits.shape, F32)
        den = exps[0] + exps[1] + exps[2] + exps[3]
        for k in range(TOP_K):
            pos = jnp.sum(jnp.where(lane == idxs[k], base, 0.0), axis=-1, keepdims=True)
            rt = jnp.where(lane == k, idxs[k], rt)
            rt = jnp.where(lane == TOP_K + k, pos.astype(I32), rt)
            gt = jnp.where(lane == k, exps[k] / den, gt)
        rows = slice(sb * ROUTE_ROWS, (sb + 1) * ROUTE_ROWS)
        rt_ref[rows, :] = rt
        gt_ref[rows, :] = gt
        cnt_ref[sb] = cnt8.astype(I32)


def out_proj(y_hy, y_att, y_s5, w_out, x, g1, n2, sc2, sh2, w_r_hi, w_r_lo, b_r):
    tm = MM_ROWS
    t = x.shape[0]
    nsub = tm // ROUTE_ROWS
    row = lambda wd: pl.BlockSpec((tm, wd), lambda i: (i, 0))
    const = lambda a, b: pl.BlockSpec((a, b), lambda i: (0, 0))
    mod_spec = pl.BlockSpec((None, 1, D_MODEL), lambda i: (_mod_row(i, tm), 0, 0))
    r = np.arange(ROUTE_ROWS)
    ltri = jnp.asarray(r[None, :] < r[:, None], BF16)
    e = np.arange(128)
    ustr = jnp.asarray(e[:, None] < e[None, :], BF16)
    return pl.pallas_call(
        _out_proj_kernel,
        grid=(t // tm,),
        in_specs=[row(HY_W), row(MLA_W), row(S5_W), const(D_MODEL, D_MODEL), row(D_MODEL),
                  mod_spec, const(1, D_MODEL), mod_spec, mod_spec,
                  const(D_MODEL, 128), const(D_MODEL, 128), const(1, 128),
                  const(ROUTE_ROWS, ROUTE_ROWS), const(128, 128)],
        out_specs=[row(D_MODEL), row(D_MODEL), row(128), row(128),
                   pl.BlockSpec((nsub, 8, 128), lambda i: (i, 0, 0))],
        out_shape=[jax.ShapeDtypeStruct((t, D_MODEL), F32),
                   jax.ShapeDtypeStruct((t, D_MODEL), BF16),
                   jax.ShapeDtypeStruct((t, 128), I32),
                   jax.ShapeDtypeStruct((t, 128), F32),
                   jax.ShapeDtypeStruct((t // ROUTE_ROWS, 8, 128), I32)],
        compiler_params=_cparams("arbitrary"),
        name="out_proj",
    )(y_hy, y_att, y_s5, w_out, x, g1, n2, sc2, sh2, w_r_hi, w_r_lo, b_r, ltri, ustr)


def route_tables(cnt_arr):
    rb = EXPERT_ROWS
    cnt = cnt_arr[:, 0, :N_EXPERTS]
    cnt = (cnt + SEG_ALIGN - 1) // SEG_ALIGN * SEG_ALIGN
    counts = jnp.sum(cnt, axis=0)
    blk_cnt = (counts + rb - 1) // rb
    blk_end = jnp.cumsum(blk_cnt)
    blk_start = blk_end - blk_cnt
    total_blk = blk_end[-1]
    seg_start = blk_start[None, :] * rb + jnp.cumsum(cnt, axis=0) - cnt
    tail = jnp.stack([total_blk * rb, (N_BLK - total_blk) * (rb // SEG_MAX)])
    pad = jnp.concatenate([blk_start * rb + counts, blk_cnt * rb - counts, tail])
    seg = dict(start=seg_start.reshape(-1).astype(I32), cnt=cnt.reshape(-1).astype(I32), pad=pad.astype(I32))

    nj = D_MODEL // EXPERT_TN
    s = jnp.arange(N_BLK * nj, dtype=I32)
    blk0 = s // nj
    valid = blk0 < total_blk

    def expert_of(blk):
        return jnp.minimum(jnp.sum((blk_end[None, :] <= blk[:, None]).astype(I32), axis=1), N_EXPERTS - 1)

    last_e = expert_of((total_blk - 1)[None])[0]
    e = expert_of(blk0)
    c_e = jnp.maximum(blk_cnt[e], 1)
    local = s - blk_start[e] * nj
    j = local // c_e
    r = local % c_e
    steps = dict(
        x_blk=jnp.where(valid, blk_start[e] + r, 0),
        o_blk=jnp.where(valid, blk_start[e] + r, blk0),
        o_j=jnp.where(valid, j, s % nj),
        w_e=jnp.where(valid, e, last_e),
        w_j=jnp.where(valid, j, nj - 1),
        first=(valid & (r == 0)).astype(I32),
        rows=jnp.where(valid, jnp.clip(counts[e] - r * rb, 0, rb), 0).astype(I32),
    )
    return seg, steps


def _aligned(row):
    return row if isinstance(row, int) else pl.multiple_of(row, SEG_ALIGN)


def _segment_copies(cnt, fn):
    for b in range(SEG_ALIGN.bit_length() - 1, SEG_MAX.bit_length()):
        size = 1 << b
        start = (cnt >> (b + 1)) << (b + 1)

        @pl.when(((cnt >> b) & 1) == 1)
        def _(start=start, size=size):
            fn(start, size)


def _dispatch_kernel(start_ref, cnt_ref, pad_ref, rt_ref, h_ref, xs_ref, buf_ref, zero_ref, sem):
    b = pl.program_id(0)
    pos_t = jnp.transpose(rt_ref[...].astype(F32)).astype(I32)
    slot = lax.broadcasted_iota(I32, (SLOT_ROWS, ROUTE_ROWS), 0)
    sel = jnp.zeros((SLOT_ROWS, ROUTE_ROWS), F32)
    for k in range(TOP_K):
        sel += (slot == pos_t[TOP_K + k:TOP_K + k + 1, :]).astype(F32)
    xs = _bdot(sel.astype(BF16), h_ref[...])

    def copy(src_ref, src, dst, size, sem_ref):
        return pltpu.make_async_copy(src_ref.at[pl.ds(_aligned(src), size)],
                                     xs_ref.at[pl.ds(_aligned(dst), size)], sem_ref)

    def for_segments(blk, fn):
        par = blk % 2

        def body(e, off):
            c = cnt_ref[blk * N_EXPERTS + e]
            dst = start_ref[blk * N_EXPERTS + e]
            _segment_copies(c, lambda st, size: fn(copy(buf_ref.at[par], off + st, dst + st, size, sem.at[par])))
            return off + c
        lax.fori_loop(0, N_EXPERTS, body, 0)

    @pl.when(b >= 2)
    def _():
        for_segments(b - 2, lambda cp: cp.wait())

    buf_ref[b % 2] = _pack_bf16(xs)
    for_segments(b, lambda cp: cp.start())

    @pl.when(b == pl.num_programs(0) - 1)
    def _():
        @pl.when(b >= 1)
        def _():
            for_segments(b - 1, lambda cp: cp.wait())
        for_segments(b, lambda cp: cp.wait())
        zero_ref[...] = jnp.zeros(zero_ref.shape, zero_ref.dtype)
        zsem = sem.at[0]

        def for_pads(fn):
            def body(e, carry):
                dst = pad_ref[e]
                _segment_copies(pad_ref[N_EXPERTS + e],
                                lambda st, size: fn(copy(zero_ref, 0, dst + st, size, zsem)))
                return carry
            lax.fori_loop(0, N_EXPERTS, body, 0)

        for_pads(lambda cp: cp.start())
        for_pads(lambda cp: cp.wait())

        tail0 = pad_ref[2 * N_EXPERTS]
        ntail = pad_ref[2 * N_EXPERTS + 1]

        def tail_copy(i):
            return copy(zero_ref, 0, tail0 + i * SEG_MAX, SEG_MAX, zsem)

        lax.fori_loop(0, ntail, lambda i, c: (tail_copy(i).start(), c)[1], 0)
        lax.fori_loop(0, ntail, lambda i, c: (tail_copy(i).wait(), c)[1], 0)


def dispatch(h2, rt, seg):
    nb = h2.shape[0] // ROUTE_ROWS
    return pl.pallas_call(
        _dispatch_kernel,
        grid_spec=pltpu.PrefetchScalarGridSpec(
            num_scalar_prefetch=3,
            grid=(nb,),
            in_specs=[pl.BlockSpec((ROUTE_ROWS, 128), lambda b, *_: (b, 0)),
                      pl.BlockSpec((ROUTE_ROWS, D_MODEL), lambda b, *_: (b, 0))],
            out_specs=pl.BlockSpec(memory_space=pl.ANY),
            scratch_shapes=[pltpu.VMEM((2, SLOT_ROWS, D_MODEL // 2), U32),
                            pltpu.VMEM((SEG_MAX, D_MODEL // 2), U32),
                            pltpu.SemaphoreType.DMA((2,))],
        ),
        out_shape=jax.ShapeDtypeStruct((N_SLOTS, D_MODEL // 2), U32),
        compiler_params=_cparams("arbitrary"),
        name="dispatch",
    )(seg["start"], seg["cnt"], seg["pad"], rt, h2)


def _pack_bf16(x):
    bits = lax.bitcast_convert_type(x.astype(BF16).astype(F32), U32)
    w = x.shape[1] // 2
    return (bits[:, w:] & jnp.uint32(0xFFFF0000)) | (bits[:, :w] >> 16)


def _unpack_bf16(words):
    lo = lax.bitcast_convert_type(words << 16, F32).astype(BF16)
    hi = lax.bitcast_convert_type(words & jnp.uint32(0xFFFF0000), F32).astype(BF16)
    return lo, hi


def _for_row_counts(rows, o_ref, fn):
    for nc in range(1, EXPERT_ROWS // EXPERT_CHUNK + 1):
        n = nc * EXPERT_CHUNK

        @pl.when((rows + EXPERT_CHUNK - 1) // EXPERT_CHUNK == nc)
        def _(n=n):
            o_ref[:n, :] = fn(n).astype(o_ref.dtype)
            if n < EXPERT_ROWS:
                o_ref[n:, :] = jnp.zeros((EXPERT_ROWS - n, o_ref.shape[1]), o_ref.dtype)

    @pl.when(rows == 0)
    def _():
        o_ref[...] = jnp.zeros(o_ref.shape, o_ref.dtype)


def _expert_up_kernel(xb_ref, ob_ref, oj_ref, we_ref, wj_ref, first_ref, rows_ref,
                      x_ref, wg_ref, wu_ref, bg_ref, bu_ref, o_ref, wgb_ref, wub_ref):
    s = pl.program_id(0)

    @pl.when(first_ref[s] == 1)
    def _():
        wgb_ref[...] = wg_ref[...].astype(BF16)
        wub_ref[...] = wu_ref[...].astype(BF16)

    def act(n):
        lo, hi = _unpack_bf16(x_ref[:n, :])
        half = D_MODEL // 2
        g = _bdot(lo, wgb_ref[:half, :]) + _bdot(hi, wgb_ref[half:, :]) + bg_ref[...]
        u = _bdot(lo, wub_ref[:half, :]) + _bdot(hi, wub_ref[half:, :]) + bu_ref[...]
        g = jnp.minimum(g, SWIGLU_LIMIT)
        u = jnp.clip(u, -SWIGLU_LIMIT, SWIGLU_LIMIT)
        return g * jax.nn.sigmoid(SWIGLU_ALPHA * g) * (u + 1.0)

    _for_row_counts(rows_ref[s], o_ref, act)


def expert_up(layer, xs, w_gu, b_gu, steps):
    tn = EXPERT_TN
    nj = D_EXPERT // tn
    n_steps = N_BLK * nj
    wspec = lambda up: pl.BlockSpec(
        (None, None, D_MODEL, tn), lambda s, xb, ob, oj, we, wj, fi, ro: (layer, we[s], 0, up * nj + wj[s]))
    bspec = lambda up: pl.BlockSpec(
        (None, None, 1, tn), lambda s, xb, ob, oj, we, wj, fi, ro: (layer, we[s], 0, up * nj + wj[s]))
    return pl.pallas_call(
        _expert_up_kernel,
        grid_spec=pltpu.PrefetchScalarGridSpec(
            num_scalar_prefetch=7,
            grid=(n_steps,),
            in_specs=[
                pl.BlockSpec((EXPERT_ROWS, D_MODEL // 2), lambda s, xb, ob, oj, we, wj, fi, ro: (xb[s], 0)),
                wspec(0), wspec(1), bspec(0), bspec(1),
            ],
            out_specs=pl.BlockSpec((EXPERT_ROWS, tn), lambda s, xb, ob, oj, we, wj, fi, ro: (ob[s], oj[s])),
            scratch_shapes=[pltpu.VMEM((D_MODEL, tn), BF16), pltpu.VMEM((D_MODEL, tn), BF16)],
        ),
        out_shape=jax.ShapeDtypeStruct((N_SLOTS, D_EXPERT), BF16),
        compiler_params=_cparams("arbitrary"),
        name="expert_up",
    )(steps["x_blk"], steps["o_blk"], steps["o_j"], steps["w_e"], steps["w_j"], steps["first"],
      steps["rows"], xs, w_gu, w_gu, b_gu, b_gu)


def _expert_down_kernel(xb_ref, ob_ref, oj_ref, we_ref, wj_ref, first_ref, rows_ref,
                        a_ref, w_ref, b_ref, o_ref, wb_ref):
    s = pl.program_id(0)

    @pl.when(first_ref[s] == 1)
    def _():
        wb_ref[...] = w_ref[...].astype(BF16)

    def down(n):
        y = _bdot(a_ref[:n, :], wb_ref[...]) + b_ref[...]
        return _pack_bf16(y)

    _for_row_counts(rows_ref[s], o_ref, down)


def expert_down(layer, act, w_dn, b_dn, steps):
    tn = EXPERT_TN
    nj = D_MODEL // tn
    n_steps = N_BLK * nj
    return pl.pallas_call(
        _expert_down_kernel,
        grid_spec=pltpu.PrefetchScalarGridSpec(
            num_scalar_prefetch=7,
            grid=(n_steps,),
            in_specs=[
                pl.BlockSpec((EXPERT_ROWS, D_EXPERT), lambda s, xb, ob, oj, we, wj, fi, ro: (xb[s], 0)),
                pl.BlockSpec((None, None, D_EXPERT, tn),
                             lambda s, xb, ob, oj, we, wj, fi, ro: (layer, we[s], 0, wj[s])),
                pl.BlockSpec((None, None, 1, tn),
                             lambda s, xb, ob, oj, we, wj, fi, ro: (layer, we[s], 0, wj[s])),
            ],
            out_specs=pl.BlockSpec((EXPERT_ROWS, tn // 2),
                                   lambda s, xb, ob, oj, we, wj, fi, ro: (ob[s], oj[s])),
            scratch_shapes=[pltpu.VMEM((D_EXPERT, tn), BF16)],
        ),
        out_shape=jax.ShapeDtypeStruct((N_SLOTS, D_MODEL // 2), U32),
        compiler_params=_cparams("arbitrary"),
        name="expert_down",
    )(steps["x_blk"], steps["o_blk"], steps["o_j"], steps["w_e"], steps["w_j"], steps["first"],
      steps["rows"], act, w_dn, b_dn)


def _combine_kernel(final, start_ref, cnt_ref, rt_ref, gt_ref, x_ref, g2_ref, nf_ref, ys_ref,
                    o_ref, buf_ref, sem):
    b = pl.program_id(0)

    def for_segments(blk, fn):
        par = blk % 2

        def body(e, off):
            c = cnt_ref[blk * N_EXPERTS + e]
            src = start_ref[blk * N_EXPERTS + e]
            _segment_copies(c, lambda st, size: fn(pltpu.make_async_copy(
                ys_ref.at[pl.ds(_aligned(src + st), size)],
                buf_ref.at[par, pl.ds(_aligned(off + st), size)], sem.at[par])))
            return off + c
        lax.fori_loop(0, N_EXPERTS, body, 0)

    @pl.when(b == 0)
    def _():
        tail = ROUTE_ROWS * TOP_K
        buf_ref[:, tail:, :] = jnp.zeros((2, SLOT_ROWS - tail, buf_ref.shape[2]), U32)
        for_segments(b, lambda cp: cp.start())

    @pl.when(b + 1 < pl.num_programs(0))
    def _():
        for_segments(b + 1, lambda cp: cp.start())

    for_segments(b, lambda cp: cp.wait())

    rt = rt_ref[...]
    gt = gt_ref[...]
    slot = lax.broadcasted_iota(I32, (ROUTE_ROWS, SLOT_ROWS), 1)
    w = jnp.zeros((ROUTE_ROWS, SLOT_ROWS), F32)
    for k in range(TOP_K):
        w += jnp.where(slot == rt[:, TOP_K + k:TOP_K + k + 1], gt[:, k:k + 1], 0.0)
    w_hi = w.astype(BF16)
    w_lo = (w - w_hi.astype(F32)).astype(BF16)
    y_lo, y_hi = _unpack_bf16(buf_ref[b % 2])
    a_lo = _bdot(w_hi, y_lo) + _bdot(w_lo, y_lo)
    a_hi = _bdot(w_hi, y_hi) + _bdot(w_lo, y_hi)
    hw = EXPERT_TN // 2
    acc = jnp.concatenate(
        [part[:, j * hw:(j + 1) * hw] for j in range(D_MODEL // EXPERT_TN) for part in (a_lo, a_hi)], axis=1)
    x2 = x_ref[...] + g2_ref[...] * acc
    if final:
        x2 = _rms(x2, nf_ref[...])
    o_ref[...] = x2


def combine(ys, rt, gates, seg, x1, g2, norm_f, final):
    tb = ROUTE_ROWS
    t = x1.shape[0]
    return pl.pallas_call(
        functools.partial(_combine_kernel, final),
        grid_spec=pltpu.PrefetchScalarGridSpec(
            num_scalar_prefetch=2,
            grid=(t // tb,),
            in_specs=[
                pl.BlockSpec((tb, 128), lambda i, *_: (i, 0)),
                pl.BlockSpec((tb, 128), lambda i, *_: (i, 0)),
                pl.BlockSpec((tb, D_MODEL), lambda i, *_: (i, 0)),
                pl.BlockSpec((None, 1, D_MODEL), lambda i, *_: (_mod_row(i, tb), 0, 0)),
                pl.BlockSpec((1, D_MODEL), lambda i, *_: (0, 0)),
                pl.BlockSpec(memory_space=pl.ANY),
            ],
            out_specs=pl.BlockSpec((tb, D_MODEL), lambda i, *_: (i, 0)),
            scratch_shapes=[pltpu.VMEM((2, SLOT_ROWS, D_MODEL // 2), U32), pltpu.SemaphoreType.DMA((2,))],
        ),
        out_shape=jax.ShapeDtypeStruct((t, D_MODEL), F32),
        compiler_params=_cparams("arbitrary"),
        name="combine",
    )(seg["start"], seg["cnt"], rt, gates, x1, g2, norm_f, ys)


def _rope_tables():
    rows = DEC_SEQ // GRID_W
    row = jnp.repeat(jnp.arange(rows, dtype=F32), GRID_W)
    col = jnp.tile(jnp.arange(GRID_W, dtype=F32), rows)
    half = QK_ROPE // 2
    inv = 1.0 / (ROPE_THETA ** (jnp.arange(0, half, 2, dtype=F32) / half))
    ar = row[:, None] * inv
    ac = col[:, None] * inv
    ang = jnp.concatenate([ar, ar, ac, ac], axis=-1)
    cos = jnp.concatenate([jnp.ones((SEQ, QK_ROPE), F32), jnp.cos(ang)], axis=0)
    sin = jnp.concatenate([jnp.zeros((SEQ, QK_ROPE), F32), jnp.sin(ang)], axis=0)
    scale = (QK_NOPE + QK_ROPE) ** -0.5
    tab_q = jnp.concatenate([jnp.ones((TAB_ROWS, QK_NOPE), F32), cos, sin], axis=-1) * scale
    tab_k = jnp.concatenate([cos, sin], axis=-1)
    return tab_q, tab_k


def _rotate_cols(w):
    r1, r2, c1, c2 = jnp.split(w, 4, axis=-1)
    return jnp.concatenate([-r2, r1, -c2, c1], axis=-1)


def _layer_weights(w_in, w_uq, w_uk, w_uv, w_out, s5_w_glu, w_router, b_router):
    hy, q, kv, kr, s5 = jnp.split(w_in, [HY_PROJ, HY_PROJ + Q_RANK, HY_PROJ + Q_RANK + KV_RANK,
                                         HY_PROJ + Q_RANK + KV_RANK + QK_ROPE], axis=-1)
    w_in_aug = jnp.concatenate([hy, q, kv, kr, _rotate_cols(kr), s5], axis=-1).astype(BF16)
    wq = w_uq.reshape(Q_RANK, MLA_HEADS, QK_NOPE + QK_ROPE)
    wq_rope = wq[..., QK_NOPE:]
    w_q = jnp.concatenate([wq[..., :QK_NOPE], wq_rope, _rotate_cols(wq_rope)], axis=-1)
    w_q = w_q.reshape(Q_RANK, MLA_HEADS * QH).astype(BF16)
    wk = w_uk.reshape(KV_RANK, MLA_HEADS, QK_NOPE)
    wk = jnp.concatenate([wk, jnp.zeros((KV_RANK, MLA_HEADS, 2 * QK_ROPE), F32)], axis=-1)
    eye = jnp.eye(QK_ROPE, dtype=F32)
    ek = jnp.concatenate([jnp.zeros((QK_ROPE, QK_NOPE), F32), eye, eye], axis=-1)
    ek = jnp.broadcast_to(ek[:, None, :], (QK_ROPE, MLA_HEADS, QH))
    wk = jnp.concatenate([wk, ek, jnp.zeros((QK_ROPE, MLA_HEADS, QH), F32)], axis=0)
    wv = jnp.concatenate([w_uv, jnp.zeros((2 * QK_ROPE, MLA_W), F32)], axis=0)
    w_kv = jnp.concatenate([wk.reshape(KVR_W, MLA_HEADS * QH), wv], axis=-1).astype(BF16)
    w_r = jnp.concatenate([w_router, jnp.zeros((D_MODEL, 128 - N_EXPERTS), F32)], axis=-1)
    b_r = jnp.concatenate([b_router, jnp.zeros((128 - N_EXPERTS,), F32)])[None, :]
    w_r_hi, w_r_lo = _split_bf16(w_r)
    return w_in_aug, w_q, w_kv, w_out.astype(BF16), s5_w_glu.astype(BF16), w_r_hi, w_r_lo, b_r


def _s5_block_diag(lbr, lbi, bbr, bbi, c_re, c_im):
    nslab = S5_NS // S5_LANES
    gps = S5_GROUPS // nslab
    eye = jnp.eye(gps, dtype=F32)
    lead = bbr.shape[:2]

    def bd_in(b):
        b = b.reshape(lead + (nslab, gps, S5_STATE, S5_GROUP_CH))
        return jnp.einsum("ldcgpn,gh->ldcgnhp", b, eye).reshape(lead + (nslab, S5_CH, S5_LANES)).astype(BF16)

    def bd_out(c):
        c = c.reshape(lead + (nslab, gps, S5_GROUP_CH, S5_STATE))
        return jnp.einsum("ldcgnp,gh->ldcgphn", c, eye).reshape(lead + (nslab, S5_LANES, S5_CH)).astype(BF16)

    lam = jnp.concatenate([lbr.reshape(lead + (1, S5_NS)), lbi.reshape(lead + (1, S5_NS))], axis=-1)
    lam8 = jnp.broadcast_to(lam, lead + (S5_SUB, 2 * S5_NS))
    return bd_in(bbr), bd_in(bbi), bd_out(c_re), bd_out(c_im), lam8


def kernel(x_prompt, x_sample, cache_ckv, cache_krope, state_ssm, c, c_ctx, w_ada, b_ada, norm1, norm2, w_in, w_out, hy_conv_w, hy_conv_b, hy_fw1, hy_fb1, hy_fw2, hy_fb2, hy_freq, hy_fw3, hy_decay, hy_skip, q_norm, kv_norm, w_uq, w_uk, w_uv, s5_lam_re, s5_lam_im, s5_log_dt, s5_b_re, s5_b_im, s5_c_re, s5_c_im, s5_d, s5_w_glu, s5_b_glu, w_router, b_router, w_gate_up, b_gate_up, w_down, b_down, norm_f):
    x = jnp.concatenate([x_prompt.reshape(T_P, D_MODEL), x_sample.reshape(T_S, D_MODEL)], axis=0)

    cond8 = jnp.concatenate([c_ctx[None, :], c, jnp.zeros((N_MOD - 1 - DEC_BATCH, D_MODEL), F32)], axis=0)
    mod = ada_modulation(cond8, w_ada, b_ada).reshape(DEPTH, N_MOD, 6, 1, D_MODEL)

    tab_q, tab_k = _rope_tables()
    dft = {}
    for n in (SEQ, DEC_SEQ):
        fwd, inv = _dft_tables(n)
        f_hi, f_lo = _split_bf16(fwd)
        dft[n] = (f_hi, f_lo, jnp.asarray(inv, F32).astype(BF16))
    fw1p = jnp.concatenate([hy_fw1, jnp.zeros((DEPTH, FILT_HID - POS_EMB, FILT_HID), F32)], axis=1)
    spectra = {n: hyena_spectra(n, fw1p, hy_fb1[:, None, :], hy_fw2, hy_fb2[:, None, :], hy_freq,
                                hy_fw3, hy_decay[:, None, :], dft[n][0], dft[n][1])
               for n in (SEQ, DEC_SEQ)}

    lbr, lbi, bbr, bbi = s5_params(s5_lam_re, s5_lam_im, s5_log_dt, s5_b_re, s5_b_im)
    s5_maps = _s5_block_diag(lbr, lbi, bbr, bbi, s5_c_re, s5_c_im)
    b_gu4 = b_gate_up.reshape(DEPTH, N_EXPERTS, 1, 2 * D_EXPERT)
    b_dn4 = b_down.reshape(DEPTH, N_EXPERTS, 1, D_MODEL)
    bp_s = S5_SUB

    ckv_out, kr_out, ssm_out = [], [], []
    for l in range(DEPTH):
        w_in_aug, w_q, w_kv, w_out_b, w_glu_b, w_r_hi, w_r_lo, b_r = _layer_weights(
            w_in[l], w_uq[l], w_uk[l], w_uv[l], w_out[l], s5_w_glu[l], w_router[l], b_router[l])
        m = lambda k: mod[l, :, k]
        z_hy, z_q, z_kvr, z_s5 = in_proj(x, norm1[l][None, :], m(1), m(0), w_in_aug)

        cw, cb, sk = hy_conv_w[l], hy_conv_b[l][None, :], hy_skip[l]
        y_hy = hyena(z_hy, cw, cb, dft[SEQ][0], dft[SEQ][2], spectra[SEQ][l], sk,
                     n=SEQ, nbatch=BATCH, row0=0, nseq=8, t_out=T_ALL,
                     prev=jnp.zeros((T_ALL, HY_W), BF16))
        y_hy = hyena(z_hy, cw, cb, dft[DEC_SEQ][0], dft[DEC_SEQ][2], spectra[DEC_SEQ][l], sk,
                     n=DEC_SEQ, nbatch=DEC_BATCH, row0=T_P, nseq=1, t_out=T_ALL, prev=y_hy)

        q = q_proj(z_q, q_norm[l][None, :], w_q, tab_q)
        a = kv_prep(z_kvr, kv_norm[l][None, :], tab_k)
        a_p = a[:T_P]
        ckv_out.append(a_p[:, :KV_RANK].reshape(BATCH, SEQ, KV_RANK))
        kr_out.append(a_p[:, KV_RANK:KV_RANK + QK_ROPE].reshape(BATCH, SEQ, QK_ROPE))
        ctx = jnp.concatenate([cache_ckv[:, l], cache_krope[:, l],
                               jnp.zeros((DEC_BATCH, PAST_LEN, QK_ROPE), F32)], axis=-1)
        a_s = jnp.concatenate([a[T_P:].reshape(DEC_BATCH, DEC_SEQ, KVR_W), ctx], axis=1)
        lk_s = DEC_SEQ + PAST_LEN
        k_p, v_p = kv_up(a_p, w_kv)
        k_s, v_s = kv_up(a_s.reshape(DEC_BATCH * lk_s, KVR_W), w_kv)
        y_att = attention(q, k_p, v_p, nbatch=BATCH, lq=SEQ, lk=SEQ, row0=0, t_out=T_ALL,
                          prev=jnp.zeros((T_ALL, MLA_W), BF16))
        y_att = attention(q, k_s, v_s, nbatch=DEC_BATCH, lq=DEC_SEQ, lk=lk_s, row0=T_P, t_out=T_ALL,
                          prev=y_att)

        st = state_ssm[:, l].reshape(DEC_BATCH, 2, 2 * S5_NS).transpose(1, 0, 2)
        x0_s = jnp.concatenate([st, jnp.zeros((2, bp_s - DEC_BATCH, 2 * S5_NS), F32)], axis=1)
        x0_p = jnp.zeros((2, BATCH, 2 * S5_NS), F32)
        y2_p, xf_p = s5_scan(z_s5, *s5_maps, x0_p, layer=l, nbatch=BATCH, n=SEQ, row0=0)
        y2_s, _ = s5_scan(z_s5, *s5_maps, x0_s, layer=l, nbatch=DEC_BATCH, n=DEC_SEQ, row0=T_P)
        ssm_out.append(xf_p.reshape(2, BATCH, 2, S5_GROUPS, S5_STATE).transpose(1, 0, 2, 3, 4))
        d_row, bg_row = s5_d[l][None, :], s5_b_glu[l][None, :]
        y_s5 = s5_glu(y2_p.reshape(2, T_P, S5_W), z_s5, d_row, w_glu_b, bg_row,
                      jnp.zeros((T_ALL, S5_W), BF16), row0=0)
        y_s5 = s5_glu(y2_s.reshape(2, T_S, S5_W), z_s5, d_row, w_glu_b, bg_row, y_s5, row0=T_P)

        x1, h2, rt, gates, cnt = out_proj(y_hy, y_att, y_s5, w_out_b, x, m(2), norm2[l][None, :],
                                          m(4), m(3), w_r_hi, w_r_lo, b_r)

        seg, steps = route_tables(cnt)
        xs = dispatch(h2, rt, seg)
        act = expert_up(l, xs, w_gate_up, b_gu4, steps)
        ys = expert_down(l, act, w_down, b_dn4, steps)
        x = combine(ys, rt, gates, seg, x1, m(5), norm_f[None, :], final=(l == DEPTH - 1))

    y_prompt = x[:T_P].reshape(BATCH, SEQ, D_MODEL)
    y_sample = x[T_P:].reshape(DEC_BATCH, DEC_SEQ, D_MODEL)
    return (y_prompt, y_sample, jnp.stack(ckv_out, axis=1), jnp.stack(kr_out, axis=1),
            jnp.stack(ssm_out, axis=1))
```

```python
import functools
import math

import numpy as np
import jax
import jax.numpy as jnp
from jax import lax
from jax.experimental import pallas as pl
from jax.experimental.pallas import tpu as pltpu

F32 = jnp.float32
BF16 = jnp.bfloat16
I32 = jnp.int32
U32 = jnp.uint32
HIGHEST = lax.Precision.HIGHEST

D_MODEL = 2048
BATCH = 32
SEQ = 256
DEPTH = 4
DEC_BATCH = 4
DEC_SEQ = 1024
PAST_LEN = 512
GRID_W = 64
EPS = 1e-6
HY_W = 512
HY_PROJ = 3 * HY_W
POS_BANDS = 16
POS_EMB = 2 * POS_BANDS + 1
FILT_HID = 64
MLA_HEADS = 8
Q_RANK = 512
KV_RANK = 256
QK_NOPE = 128
QK_ROPE = 64
V_DIM = 128
MLA_W = MLA_HEADS * V_DIM
ROPE_THETA = 10000.0
S5_W = 512
S5_GROUP_CH = 16
S5_GROUPS = 32
S5_STATE = 64
S5_NS = S5_GROUPS * S5_STATE
N_EXPERTS = 32
TOP_K = 4
D_EXPERT = D_MODEL
SWIGLU_LIMIT = 7.0
SWIGLU_ALPHA = 1.702

T_P = BATCH * SEQ
T_S = DEC_BATCH * DEC_SEQ
T_ALL = T_P + T_S
N_MOD = 8
QH = 256
KVR_W = KV_RANK + 2 * QK_ROPE
ROW_TILE = 256
TAB_ROWS = SEQ + DEC_SEQ

VMEM_LIMIT_BYTES = 56 * 1024 * 1024

EXPERT_ROWS = 512
N_ASSIGN = T_ALL * TOP_K
ROUTE_ROWS = 256
N_RBLK = T_ALL // ROUTE_ROWS
SEG_ALIGN = 8
SEG_MAX = ROUTE_ROWS
SLOT_ROWS = ROUTE_ROWS * TOP_K + N_EXPERTS * SEG_ALIGN
N_BLK = (N_ASSIGN + N_RBLK * N_EXPERTS * (SEG_ALIGN - 1)) // EXPERT_ROWS + 1 + N_EXPERTS
N_SLOTS = N_BLK * EXPERT_ROWS
EXPERT_TN = 1024
EXPERT_CHUNK = 128
MM_ROWS = 512


def _cparams(*sem):
    return pltpu.CompilerParams(dimension_semantics=sem, vmem_limit_bytes=VMEM_LIMIT_BYTES)


def _mod_row(i, tm):
    npb = T_P // tm
    sb = DEC_SEQ // tm
    return jnp.where(i < npb, 0, 1 + (i - npb) // sb)


def _tab_blk(i, tm):
    npb = T_P // tm
    pb = SEQ // tm
    sb = DEC_SEQ // tm
    return jnp.where(i < npb, i % pb, pb + (i - npb) % sb)


def _rms(x, g):
    return x * lax.rsqrt(jnp.mean(x * x, axis=-1, keepdims=True) + EPS) * g


def _bdot(a, b):
    return jnp.dot(a, b, preferred_element_type=F32)


def _ada_kernel(c_ref, w_ref, b_ref, o_ref):
    s = jax.nn.silu(c_ref[...]).astype(BF16)
    o_ref[...] = _bdot(s, w_ref[...].astype(BF16)) + b_ref[...]


def ada_modulation(cond8, w_ada, b_ada):
    tn = 1024
    n = 6 * D_MODEL
    return pl.pallas_call(
        _ada_kernel,
        grid=(DEPTH, n // tn),
        in_specs=[
            pl.BlockSpec((N_MOD, D_MODEL), lambda l, j: (0, 0)),
            pl.BlockSpec((None, D_MODEL, tn), lambda l, j: (l, 0, j)),
            pl.BlockSpec((None, 1, tn), lambda l, j: (l, 0, j)),
        ],
        out_specs=pl.BlockSpec((None, N_MOD, tn), lambda l, j: (l, 0, j)),
        out_shape=jax.ShapeDtypeStruct((DEPTH, N_MOD, n), F32),
        compiler_params=_cparams("arbitrary", "arbitrary"),
        name="ada_modulation",
    )(cond8, w_ada, b_ada.reshape(DEPTH, 1, n))


IN_COLS = (HY_PROJ, Q_RANK, KVR_W, S5_W)
IN_W_AUG = sum(IN_COLS)


def _in_proj_kernel(x_ref, g_ref, sc_ref, sh_ref, w_ref, zhy_ref, zq_ref, zkvr_ref, zs5_ref):
    h = _rms(x_ref[...], g_ref[...]) * (1.0 + sc_ref[...]) + sh_ref[...]
    h = h.astype(BF16)
    off = 0
    for o_ref, wd in zip((zhy_ref, zq_ref, zkvr_ref, zs5_ref), IN_COLS):
        o_ref[...] = _bdot(h, w_ref[:, off:off + wd])
        off += wd


def in_proj(x, g, sc, sh, w_aug):
    tm = MM_ROWS
    t = x.shape[0]
    mod_spec = pl.BlockSpec((None, 1, D_MODEL), lambda i: (_mod_row(i, tm), 0, 0))
    return pl.pallas_call(
        _in_proj_kernel,
        grid=(t // tm,),
        in_specs=[
            pl.BlockSpec((tm, D_MODEL), lambda i: (i, 0)),
            pl.BlockSpec((1, D_MODEL), lambda i: (0, 0)),
            mod_spec, mod_spec,
            pl.BlockSpec((D_MODEL, IN_W_AUG), lambda i: (0, 0)),
        ],
        out_specs=[pl.BlockSpec((tm, wd), lambda i: (i, 0)) for wd in IN_COLS],
        out_shape=[jax.ShapeDtypeStruct((t, wd), F32) for wd in IN_COLS],
        compiler_params=_cparams("arbitrary"),
        name="in_proj",
    )(x, g, sc, sh, w_aug)


def _dft_tables(n):
    f = np.arange(n)[:, None]
    s = np.arange(n)[None, :]
    ang = (np.pi / n) * ((f * s) % (2 * n)).astype(np.float64)
    cos = np.cos(ang)
    sin = np.sin(ang)
    nyq = np.where(np.arange(n) % 2 == 0, 1.0, -1.0)
    fs = sin.copy()
    fs[0, :] = nyq
    fwd = np.concatenate([cos, fs], axis=0)
    wgt = np.full((n,), 2.0)
    wgt[0] = 1.0
    gc = (cos * wgt[:, None]).T / (2 * n)
    gs = (sin * wgt[:, None]).T / (2 * n)
    gs[:, 0] = nyq / (2 * n)
    inv = np.concatenate([gc, gs], axis=1)
    return fwd, inv


def _split_bf16(a):
    hi = jnp.asarray(a, F32).astype(BF16)
    lo = (jnp.asarray(a, F32) - hi.astype(F32)).astype(BF16)
    return hi, lo


def _hyena_spec_kernel(n, feat_ref, t_ref, fw1_ref, fb1_ref, fw2_ref, fb2_ref, freq_ref,
                       w3f0_ref, w3f1_ref, w3b0_ref, w3b1_ref,
                       dcf0_ref, dcf1_ref, dcb0_ref, dcb1_ref,
                       fhi_ref, flo_ref, o_ref):
    hdot = functools.partial(jnp.dot, precision=HIGHEST, preferred_element_type=F32)
    h = jnp.sin(freq_ref[0:1, :] * (hdot(feat_ref[...], fw1_ref[...]) + fb1_ref[...]))
    h = jnp.sin(freq_ref[1:2, :] * (hdot(h, fw2_ref[...]) + fb2_ref[...]))
    tcol = t_ref[...]
    row0 = lax.broadcasted_iota(I32, (n, 1), 0) == 0

    def filt(w3_ref, dc_ref):
        return hdot(h, w3_ref[...]) * jnp.exp(-tcol * jnp.abs(dc_ref[...]))

    def dft(fpart_hi, fpart_lo, k):
        k_hi = k.astype(BF16)
        k_lo = (k - k_hi.astype(F32)).astype(BF16)
        return _bdot(fpart_hi, k_hi) + (_bdot(fpart_hi, k_lo) + _bdot(fpart_lo, k_hi))

    for o, (wf, wb, df, db) in enumerate(((w3f0_ref, w3b0_ref, dcf0_ref, dcb0_ref),
                                          (w3f1_ref, w3b1_ref, dcf1_ref, dcb1_ref))):
        fwd = filt(wf, df)
        bwd = jnp.where(row0, 0.0, filt(wb, db))
        ks = dft(fhi_ref[...], flo_ref[...], fwd + bwd)
        kd = dft(fhi_ref[n:, :], flo_ref[n:, :], fwd - bwd)
        kr = ks[:n]
        ksp = jnp.where(row0, ks[n:n + 1], kd)
        o_ref[3 * o + 0] = kr
        o_ref[3 * o + 1] = jnp.where(row0, 0.0, ksp)
        o_ref[3 * o + 2] = jnp.where(row0, ksp, kr)


def hyena_spectra(n, fw1p, fb1, fw2, fb2, freq, fw3, decay, fwd_hi, fwd_lo):
    tc = 256
    nc = HY_W // tc
    f32 = F32
    t = jnp.linspace(0.0, 1.0, n, dtype=f32)[:, None]
    w = (2.0 * math.pi / n) * jnp.arange(n, dtype=f32)[:, None]
    bands = jnp.linspace(1e-4, POS_BANDS - 1, POS_BANDS, dtype=f32)
    feat = jnp.concatenate([t, jnp.cos(w * bands), -jnp.sin(w * bands),
                            jnp.zeros((n, FILT_HID - POS_EMB), f32)], axis=-1)

    def w3_spec(d, o):
        return pl.BlockSpec((None, FILT_HID, tc), lambda l, c: (l, 0, (d * 2 + o) * nc + c))

    def dc_spec(d, o):
        return pl.BlockSpec((None, 1, tc), lambda l, c: (l, 0, (d * 2 + o) * nc + c))

    const2 = lambda shape: pl.BlockSpec(shape, lambda l, c: (0, 0))
    lay3 = lambda a, b: pl.BlockSpec((None, a, b), lambda l, c: (l, 0, 0))
    return pl.pallas_call(
        functools.partial(_hyena_spec_kernel, n),
        grid=(DEPTH, nc),
        in_specs=[
            const2((n, FILT_HID)), const2((n, 1)),
            lay3(FILT_HID, FILT_HID), lay3(1, FILT_HID), lay3(FILT_HID, FILT_HID), lay3(1, FILT_HID),
            lay3(2, FILT_HID),
            w3_spec(0, 0), w3_spec(0, 1), w3_spec(1, 0), w3_spec(1, 1),
            dc_spec(0, 0), dc_spec(0, 1), dc_spec(1, 0), dc_spec(1, 1),
            const2((2 * n, n)), const2((2 * n, n)),
        ],
        out_specs=pl.BlockSpec((None, 6, n, tc), lambda l, c: (l, 0, 0, c)),
        out_shape=jax.ShapeDtypeStruct((DEPTH, 6, n, HY_W), F32),
        compiler_params=_cparams("arbitrary", "arbitrary"),
        name=f"hyena_spectra_{n}",
    )(feat, t, fw1p, fb1, fw2, fb2, freq, fw3, fw3, fw3, fw3, decay, decay, decay, decay,
      fwd_hi, fwd_lo)


def _hyena_kernel(n, nseq, zv_ref, z1_ref, z2_ref, wv_ref, w1_ref, w2_ref, bv_ref, b1_ref, b2_ref,
                  f_ref, g_ref, spec_ref, skip_ref, *rest):
    o_ref = rest[-1]
    row = lax.broadcasted_iota(I32, (n, 1), 0)
    first = row == 0
    last = row == n - 1

    def conv3(z, w_ref, b_ref):
        zm = jnp.where(first, 0.0, pltpu.roll(z, 1, 0))
        zp = jnp.where(last, 0.0, pltpu.roll(z, n - 1, 0))
        return zm * w_ref[0:1, :] + z * w_ref[1:2, :] + zp * w_ref[2:3, :] + b_ref[...]

    def long_conv(u, o):
        uu = _bdot(f_ref[...], u.astype(BF16))
        a = uu[:n]
        b = uu[n:]
        kr = spec_ref[3 * o + 0]
        ks = spec_ref[3 * o + 1]
        dd = spec_ref[3 * o + 2]
        yre = (kr * a - ks * b).astype(BF16)
        zz = (dd * b + ks * a).astype(BF16)
        y = _bdot(g_ref[:, :n], yre) + _bdot(g_ref[:, n:], zz)
        return y + u * skip_ref[o:o + 1, :]

    for s in range(nseq):
        rows = pl.ds(s * n, n)
        v = conv3(zv_ref[rows, :], wv_ref, bv_ref)
        x1 = conv3(z1_ref[rows, :], w1_ref, b1_ref)
        x2 = conv3(z2_ref[rows, :], w2_ref, b2_ref)
        y = x1 * long_conv(v, 0)
        y = x2 * long_conv(y, 1)
        o_ref[rows, :] = y.astype(o_ref.dtype)


def hyena(z_hy, conv_w, conv_b, f_hi, g_inv, spec, skip, *, n, nbatch, row0, nseq, t_out, prev=None):
    tc = 256
    nc = HY_W // tc
    rb = nseq * n
    blk0 = row0 // rb
    assert row0 % rb == 0 and nbatch % nseq == 0

    def zspec(part):
        return pl.BlockSpec((rb, tc), lambda c, b: (blk0 + b, part * nc + c))

    def wspec(part, rows):
        return pl.BlockSpec((rows, tc), lambda c, b: (0, part * nc + c))

    in_specs = [zspec(0), zspec(1), zspec(2), wspec(0, 3), wspec(1, 3), wspec(2, 3),
                wspec(0, 1), wspec(1, 1), wspec(2, 1),
                pl.BlockSpec((2 * n, n), lambda c, b: (0, 0)),
                pl.BlockSpec((n, 2 * n), lambda c, b: (0, 0)),
                pl.BlockSpec((6, n, tc), lambda c, b: (0, 0, c)),
                pl.BlockSpec((2, tc), lambda c, b: (0, c))]
    args = [z_hy, z_hy, z_hy, conv_w, conv_w, conv_w, conv_b, conv_b, conv_b, f_hi, g_inv, spec, skip]
    aliases = {}
    if prev is not None:
        in_specs.append(pl.BlockSpec(memory_space=pl.ANY))
        args.append(prev)
        aliases = {len(args) - 1: 0}
    return pl.pallas_call(
        functools.partial(_hyena_kernel, n, nseq),
        grid=(nc, nbatch // nseq),
        in_specs=in_specs,
        out_specs=pl.BlockSpec((rb, tc), lambda c, b: (blk0 + b, c)),
        out_shape=jax.ShapeDtypeStruct((t_out, HY_W), BF16),
        input_output_aliases=aliases,
        compiler_params=_cparams("arbitrary", "arbitrary"),
        name=f"hyena_{n}",
    )(*args)


def _q_proj_kernel(z_ref, g_ref, w_ref, tab_ref, o_ref):
    h = _rms(z_ref[...], g_ref[...]).astype(BF16)
    tab = tab_ref[...]
    for hd in range(MLA_HEADS):
        cols = slice(hd * QH, (hd + 1) * QH)
        o_ref[:, cols] = (_bdot(h, w_ref[:, cols]) * tab).astype(BF16)


def q_proj(z_q, g, w_q, tab_q):
    tm = ROW_TILE
    t = z_q.shape[0]
    return pl.pallas_call(
        _q_proj_kernel,
        grid=(t // tm,),
        in_specs=[
            pl.BlockSpec((tm, Q_RANK), lambda i: (i, 0)),
            pl.BlockSpec((1, Q_RANK), lambda i: (0, 0)),
            pl.BlockSpec((Q_RANK, MLA_HEADS * QH), lambda i: (0, 0)),
            pl.BlockSpec((tm, QH), lambda i: (_tab_blk(i, tm), 0)),
        ],
        out_specs=pl.BlockSpec((tm, MLA_HEADS * QH), lambda i: (i, 0)),
        out_shape=jax.ShapeDtypeStruct((t, MLA_HEADS * QH), BF16),
        compiler_params=_cparams("arbitrary"),
        name="q_proj",
    )(z_q, g, w_q, tab_q)


def _kv_prep_kernel(z_ref, g_ref, tab_ref, o_ref):
    o_ref[:, :KV_RANK] = _rms(z_ref[:, :KV_RANK], g_ref[...])
    t = z_ref[:, KV_RANK:] * tab_ref[...]
    o_ref[:, KV_RANK:] = t + pltpu.roll(t, QK_ROPE, 1)


def kv_prep(z_kvr, g, tab_k):
    tm = ROW_TILE
    t = z_kvr.shape[0]
    return pl.pallas_call(
        _kv_prep_kernel,
        grid=(t // tm,),
        in_specs=[
            pl.BlockSpec((tm, KVR_W), lambda i: (i, 0)),
            pl.BlockSpec((1, KV_RANK), lambda i: (0, 0)),
            pl.BlockSpec((tm, 2 * QK_ROPE), lambda i: (_tab_blk(i, tm), 0)),
        ],
        out_specs=pl.BlockSpec((tm, KVR_W), lambda i: (i, 0)),
        out_shape=jax.ShapeDtypeStruct((t, KVR_W), F32),
        compiler_params=_cparams("arbitrary"),
        name="kv_prep",
    )(z_kvr, g, tab_k)


def _kv_up_kernel(a_ref, w_ref, k_ref, v_ref):
    a = a_ref[...].astype(BF16)
    nk = MLA_HEADS * QH
    k_ref[...] = _bdot(a, w_ref[:, :nk]).astype(BF16)
    v_ref[...] = _bdot(a, w_ref[:, nk:]).astype(BF16)


def kv_up(a, w_kv):
    tm = 512
    t = a.shape[0]
    nk = MLA_HEADS * QH
    return pl.pallas_call(
        _kv_up_kernel,
        grid=(t // tm,),
        in_specs=[
            pl.BlockSpec((tm, KVR_W), lambda i: (i, 0)),
            pl.BlockSpec((KVR_W, nk + MLA_W), lambda i: (0, 0)),
        ],
        out_specs=[pl.BlockSpec((tm, nk), lambda i: (i, 0)),
                   pl.BlockSpec((tm, MLA_W), lambda i: (i, 0))],
        out_shape=[jax.ShapeDtypeStruct((t, nk), BF16), jax.ShapeDtypeStruct((t, MLA_W), BF16)],
        compiler_params=_cparams("arbitrary"),
        name="kv_up",
    )(a, w_kv)


def _attn_kernel(q_ref, k_ref, v_ref, *rest):
    o_ref = rest[-1]
    for hd in range(MLA_HEADS):
        q = q_ref[:, hd * QH:(hd + 1) * QH]
        k = k_ref[:, hd * QH:(hd + 1) * QH]
        s = lax.dot_general(q, k, (((1,), (1,)), ((), ())), preferred_element_type=F32)
        m = jnp.max(s, axis=-1, keepdims=True)
        p = jnp.exp(s - m)
        l = jnp.sum(p, axis=-1, keepdims=True)
        o = _bdot(p.astype(BF16), v_ref[:, hd * V_DIM:(hd + 1) * V_DIM])
        o_ref[:, hd * V_DIM:(hd + 1) * V_DIM] = (o / l).astype(o_ref.dtype)


def attention(q, k, v, *, nbatch, lq, lk, row0, t_out, prev=None):
    tq = ROW_TILE
    nq = lq // tq
    blk0 = row0 // tq
    in_specs = [
        pl.BlockSpec((tq, MLA_HEADS * QH), lambda b, i: (blk0 + b * nq + i, 0)),
        pl.BlockSpec((lk, MLA_HEADS * QH), lambda b, i: (b, 0)),
        pl.BlockSpec((lk, MLA_W), lambda b, i: (b, 0)),
    ]
    args = [q, k, v]
    aliases = {}
    if prev is not None:
        in_specs.append(pl.BlockSpec(memory_space=pl.ANY))
        args.append(prev)
        aliases = {3: 0}
    return pl.pallas_call(
        _attn_kernel,
        grid=(nbatch, nq),
        in_specs=in_specs,
        out_specs=pl.BlockSpec((tq, MLA_W), lambda b, i: (blk0 + b * nq + i, 0)),
        out_shape=jax.ShapeDtypeStruct((t_out, MLA_W), BF16),
        input_output_aliases=aliases,
        compiler_params=_cparams("arbitrary", "arbitrary"),
        name=f"attention_{lk}",
    )(*args)


def _s5_param_kernel(lr_ref, li_ref, dt_ref, lrx_ref, lix_ref, dtx_ref, bre_ref, bim_ref,
                     lbr_ref, lbi_ref, bbr_ref, bbi_ref):
    def lam_bar(lr, li, ldt):
        lr = jnp.minimum(lr, -1e-4)
        dt = jnp.exp(ldt)
        e = jnp.exp(lr * dt)
        return lr, e * jnp.cos(li * dt), e * jnp.sin(li * dt)

    _, lbr, lbi = lam_bar(lr_ref[...], li_ref[...], dt_ref[...])
    lbr_ref[...] = lbr
    lbi_ref[...] = lbi
    lr, xr, xi = lam_bar(lrx_ref[...], lix_ref[...], dtx_ref[...])
    li = lix_ref[...]
    nr = xr - 1.0
    den = lr * lr + li * li
    cr = (nr * lr + xi * li) / den
    ci = (xi * lr - nr * li) / den
    bbr_ref[...] = cr * bre_ref[...] - ci * bim_ref[...]
    bbi_ref[...] = cr * bim_ref[...] + ci * bre_ref[...]


def s5_params(lam_re, lam_im, log_dt, b_re, b_im):
    r = DEPTH * 2 * S5_GROUPS
    ldt = jnp.broadcast_to(log_dt[..., None], lam_re.shape)
    small = [a.reshape(r, S5_STATE) for a in (lam_re, lam_im, ldt)]
    wide = [jnp.repeat(a, S5_GROUP_CH, axis=-1) for a in small]
    bs = [a.reshape(r, S5_STATE * S5_GROUP_CH) for a in (b_re, b_im)]
    outs = pl.pallas_call(
        _s5_param_kernel,
        out_shape=[jax.ShapeDtypeStruct((r, S5_STATE), F32)] * 2
        + [jax.ShapeDtypeStruct((r, S5_STATE * S5_GROUP_CH), F32)] * 2,
        name="s5_params",
    )(*small, *wide, *bs)
    lbr, lbi, bbr, bbi = outs
    shp = (DEPTH, 2, S5_GROUPS, S5_STATE)
    return (lbr.reshape(shp), lbi.reshape(shp),
            bbr.reshape(shp + (S5_GROUP_CH,)), bbi.reshape(shp + (S5_GROUP_CH,)))


S5_SUB = 8
S5_LANES = 512
S5_CH = S5_LANES // S5_STATE * S5_GROUP_CH


def _s5_scan_kernel(tq, nseq, u_ref, pin_ref, pout_ref, bre_ref, bim_ref, cre_ref, cim_ref, lam_ref, x0_ref,
                    y_ref, xf_ref, s_ref, x_ref):
    d = pl.program_id(0)
    j = pl.program_id(2)

    @pl.when(j == 0)
    def _():
        x_ref[...] = x0_ref[...]

    u = u_ref[...].reshape(nseq * tq, S5_W).astype(BF16)
    u = _bdot(pin_ref[...], u).astype(BF16)
    for c in range(S5_NS // S5_LANES):
        ch = slice(c * S5_CH, (c + 1) * S5_CH)
        st = slice(c * S5_LANES, (c + 1) * S5_LANES)
        s_ref[:, st] = _bdot(u[:, ch], bre_ref[c])
        s_ref[:, S5_NS + c * S5_LANES:S5_NS + (c + 1) * S5_LANES] = _bdot(u[:, ch], bim_ref[c])

    for c in range(S5_NS // S5_LANES):
        re = pl.ds(c * S5_LANES, S5_LANES)
        im = pl.ds(S5_NS + c * S5_LANES, S5_LANES)
        lr = lam_ref[:, re]
        li = lam_ref[:, im]

        def step(i, carry):
            xr, xi = carry
            t = jnp.where(d == 0, i, tq - 1 - i)
            r = pl.ds(pl.multiple_of(t * S5_SUB, S5_SUB), S5_SUB)
            nr = lr * xr - li * xi + s_ref[r, re]
            ni = lr * xi + li * xr + s_ref[r, im]
            s_ref[r, re] = nr
            s_ref[r, im] = ni
            return nr, ni

        xr, xi = lax.fori_loop(0, tq, step, (x_ref[:, re], x_ref[:, im]), unroll=4)
        x_ref[:, re] = xr
        x_ref[:, im] = xi

    for c in range(S5_NS // S5_LANES):
        ch = slice(c * S5_CH, (c + 1) * S5_CH)
        st = slice(c * S5_LANES, (c + 1) * S5_LANES)
        y = (_bdot(s_ref[:, st].astype(BF16), cre_ref[c])
             - _bdot(s_ref[:, S5_NS + c * S5_LANES:S5_NS + (c + 1) * S5_LANES].astype(BF16), cim_ref[c]))
        y_hi = y.astype(BF16)
        y_lo = (y - y_hi.astype(F32)).astype(BF16)
        y = _bdot(pout_ref[...], y_hi) + _bdot(pout_ref[...], y_lo)
        y_ref[:, :, ch] = y.reshape(nseq, tq, S5_CH)

    @pl.when(j == pl.num_programs(2) - 1)
    def _():
        xf_ref[...] = x_ref[...]


def s5_scan(z_s5, b_re, b_im, c_re, c_im, lam8, x0, *, layer, nbatch, n, row0):
    tq = 64
    nch = n // tq
    nseq = min(nbatch, S5_SUB)
    u = z_s5.reshape(-1, n, S5_W)
    g0 = row0 // n // nseq
    assert row0 % (n * nseq) == 0 and nbatch % nseq == 0
    t_i, b_i = np.meshgrid(np.arange(tq), np.arange(nseq), indexing="ij")
    pin = np.zeros((tq * S5_SUB, nseq * tq), np.float32)
    pin[(t_i * S5_SUB + b_i).ravel(), (b_i * tq + t_i).ravel()] = 1.0
    pin_b = jnp.asarray(pin, BF16)
    pout_b = jnp.asarray(pin.T, BF16)

    def tci(d, j):
        return d * (nch - 1) + (1 - 2 * d) * j

    nslab = S5_NS // S5_LANES
    slabs = lambda a, b: pl.BlockSpec((None, None, nslab, a, b), lambda d, g, j: (layer, d, 0, 0, 0))
    const = lambda a, b: pl.BlockSpec((a, b), lambda d, g, j: (0, 0))
    return pl.pallas_call(
        functools.partial(_s5_scan_kernel, tq, nseq),
        grid=(2, nbatch // nseq, nch),
        in_specs=[
            pl.BlockSpec((nseq, tq, S5_W), lambda d, g, j: (g0 + g, tci(d, j), 0)),
            const(tq * S5_SUB, nseq * tq), const(nseq * tq, tq * S5_SUB),
            slabs(S5_CH, S5_LANES), slabs(S5_CH, S5_LANES), slabs(S5_LANES, S5_CH), slabs(S5_LANES, S5_CH),
            pl.BlockSpec((None, None, S5_SUB, 2 * S5_NS), lambda d, g, j: (layer, d, 0, 0)),
            pl.BlockSpec((None, S5_SUB, 2 * S5_NS), lambda d, g, j: (d, g, 0)),
        ],
        out_specs=[
            pl.BlockSpec((None, nseq, tq, S5_W), lambda d, g, j: (d, g, tci(d, j), 0)),
            pl.BlockSpec((None, S5_SUB, 2 * S5_NS), lambda d, g, j: (d, g, 0)),
        ],
        out_shape=[jax.ShapeDtypeStruct((2, nbatch, n, S5_W), F32),
                   jax.ShapeDtypeStruct((2, x0.shape[1], 2 * S5_NS), F32)],
        scratch_shapes=[pltpu.VMEM((tq * S5_SUB, 2 * S5_NS), F32),
                        pltpu.VMEM((S5_SUB, 2 * S5_NS), F32)],
        compiler_params=_cparams("arbitrary", "arbitrary", "arbitrary"),
        name=f"s5_scan_{n}",
    )(u, pin_b, pout_b, b_re, b_im, c_re, c_im, lam8, x0)


def _s5_glu_kernel(y_ref, u_ref, d_ref, w_ref, b_ref, prev_ref, o_ref):
    y = y_ref[0] + y_ref[1] + u_ref[...] * d_ref[...]
    zg = jax.nn.gelu(y)
    gate = jax.nn.sigmoid(_bdot(zg.astype(BF16), w_ref[...]) + b_ref[...])
    o_ref[...] = (zg * gate).astype(o_ref.dtype)


def s5_glu(y2, z_s5, d, w_glu, b_glu, prev, *, row0):
    tm = 512
    r = y2.shape[1]
    blk0 = row0 // tm
    return pl.pallas_call(
        _s5_glu_kernel,
        grid=(r // tm,),
        in_specs=[
            pl.BlockSpec((2, tm, S5_W), lambda i: (0, i, 0)),
            pl.BlockSpec((tm, S5_W), lambda i: (blk0 + i, 0)),
            pl.BlockSpec((1, S5_W), lambda i: (0, 0)),
            pl.BlockSpec((S5_W, S5_W), lambda i: (0, 0)),
            pl.BlockSpec((1, S5_W), lambda i: (0, 0)),
            pl.BlockSpec(memory_space=pl.ANY),
        ],
        out_specs=pl.BlockSpec((tm, S5_W), lambda i: (blk0 + i, 0)),
        out_shape=jax.ShapeDtypeStruct(prev.shape, BF16),
        input_output_aliases={5: 0},
        compiler_params=_cparams("arbitrary"),
        name="s5_glu",
    )(y2, z_s5, d, w_glu, b_glu, prev)


def _out_proj_kernel(yh_ref, ya_ref, ys_ref, w_ref, x_ref, g1_ref, n2_ref, sc_ref, sh_ref,
                     wrh_ref, wrl_ref, br_ref, ltri_ref, ustr_ref,
                     x1_ref, h2_ref, rt_ref, gt_ref, cnt_ref):
    acc = _bdot(yh_ref[...], w_ref[:HY_W, :])
    acc += _bdot(ya_ref[...], w_ref[HY_W:HY_W + MLA_W, :])
    acc += _bdot(ys_ref[...], w_ref[HY_W + MLA_W:, :])
    x1 = x_ref[...] + g1_ref[...] * acc
    x1_ref[...] = x1
    h2 = _rms(x1, n2_ref[...]) * (1.0 + sc_ref[...]) + sh_ref[...]
    h_hi = h2.astype(BF16)
    h2_ref[...] = h_hi

    h_lo = (h2 - h_hi.astype(F32)).astype(BF16)
    logits_all = (_bdot(h_hi, wrh_ref[...]) + (_bdot(h_hi, wrl_ref[...]) + _bdot(h_lo, wrh_ref[...]))
                  + br_ref[...])
    neg = jnp.float32(-jnp.inf)
    nsub = x_ref.shape[0] // ROUTE_ROWS
    for sb in range(nsub):
        logits = logits_all[sb * ROUTE_ROWS:(sb + 1) * ROUTE_ROWS]
        lane = lax.broadcasted_iota(I32, logits.shape, 1)
        logits = jnp.where(lane < N_EXPERTS, logits, neg)
        idxs, exps = [], []
        v0 = None
        for k in range(TOP_K):
            m = jnp.max(logits, axis=-1, keepdims=True)
            idx = jnp.min(jnp.where(logits == m, lane, 128), axis=-1, keepdims=True)
            if k == 0:
                v0 = m
            idxs.append(idx)
            exps.append(jnp.exp(m - v0))
            logits = jnp.where(lane == idx, neg, logits)
        ohs = jnp.zeros(logits.shape, F32)
        for idx in idxs:
            ohs += (lane == idx).astype(F32)
        before = _bdot(ltri_ref[...], ohs.astype(BF16))
        cnt = jnp.sum(ohs, axis=0, keepdims=True)
        cnt8 = jnp.broadcast_to(cnt, (8, 128))
        seg = jnp.floor((cnt8 + (SEG_ALIGN - 1)) * (1.0 / SEG_ALIGN)) * SEG_ALIGN
        lower = _bdot(seg.astype(BF16), ustr_ref[...])[0:1]
        base = before + lower
        rt = jnp.zeros(logits.shape, I32)
        gt = jnp.zeros(logits.shape, F32)
        den = exps[0] + exps[1] + exps[2] + exps[3]
        for k in range(TOP_K):
            pos = jnp.sum(jnp.where(lane == idxs[k], base, 0.0), axis=-1, keepdims=True)
            rt = jnp.where(lane == k, idxs[k], rt)
            rt = jnp.where(lane == TOP_K + k, pos.astype(I32), rt)
            gt = jnp.where(lane == k, exps[k] / den, gt)
        rows = slice(sb * ROUTE_ROWS, (sb + 1) * ROUTE_ROWS)
        rt_ref[rows, :] = rt
        gt_ref[rows, :] = gt
        cnt_ref[sb] = cnt8.astype(I32)


def out_proj(y_hy, y_att, y_s5, w_out, x, g1, n2, sc2, sh2, w_r_hi, w_r_lo, b_r):
    tm = MM_ROWS
    t = x.shape[0]
    nsub = tm // ROUTE_ROWS
    row = lambda wd: pl.BlockSpec((tm, wd), lambda i: (i, 0))
    const = lambda a, b: pl.BlockSpec((a, b), lambda i: (0, 0))
    mod_spec = pl.BlockSpec((None, 1, D_MODEL), lambda i: (_mod_row(i, tm), 0, 0))
    r = np.arange(ROUTE_ROWS)
    ltri = jnp.asarray(r[None, :] < r[:, None], BF16)
    e = np.arange(128)
    ustr = jnp.asarray(e[:, None] < e[None, :], BF16)
    return pl.pallas_call(
        _out_proj_kernel,
        grid=(t // tm,),
        in_specs=[row(HY_W), row(MLA_W), row(S5_W), const(D_MODEL, D_MODEL), row(D_MODEL),
                  mod_spec, const(1, D_MODEL), mod_spec, mod_spec,
                  const(D_MODEL, 128), const(D_MODEL, 128), const(1, 128),
                  const(ROUTE_ROWS, ROUTE_ROWS), const(128, 128)],
        out_specs=[row(D_MODEL), row(D_MODEL), row(128), row(128),
                   pl.BlockSpec((nsub, 8, 128), lambda i: (i, 0, 0))],
        out_shape=[jax.ShapeDtypeStruct((t, D_MODEL), F32),
                   jax.ShapeDtypeStruct((t, D_MODEL), BF16),
                   jax.ShapeDtypeStruct((t, 128), I32),
                   jax.ShapeDtypeStruct((t, 128), F32),
                   jax.ShapeDtypeStruct((t // ROUTE_ROWS, 8, 128), I32)],
        compiler_params=_cparams("arbitrary"),
        name="out_proj",
    )(y_hy, y_att, y_s5, w_out, x, g1, n2, sc2, sh2, w_r_hi, w_r_lo, b_r, ltri, ustr)


def route_tables(cnt_arr):
    rb = EXPERT_ROWS
    cnt = cnt_arr[:, 0, :N_EXPERTS]
    cnt = (cnt + SEG_ALIGN - 1) // SEG_ALIGN * SEG_ALIGN
    counts = jnp.sum(cnt, axis=0)
    blk_cnt = (counts + rb - 1) // rb
    blk_end = jnp.cumsum(blk_cnt)
    blk_start = blk_end - blk_cnt
    total_blk = blk_end[-1]
    seg_start = blk_start[None, :] * rb + jnp.cumsum(cnt, axis=0) - cnt
    tail = jnp.stack([total_blk * rb, (N_BLK - total_blk) * (rb // SEG_MAX)])
    pad = jnp.concatenate([blk_start * rb + counts, blk_cnt * rb - counts, tail])
    cnt_tab = jnp.concatenate([cnt.reshape(-1), jnp.sum(cnt, axis=1)])
    seg = dict(start=seg_start.reshape(-1).astype(I32), cnt=cnt_tab.astype(I32), pad=pad.astype(I32))

    nj = D_MODEL // EXPERT_TN
    s = jnp.arange(N_BLK * nj, dtype=I32)
    blk0 = s // nj
    valid = blk0 < total_blk

    def expert_of(blk):
        return jnp.minimum(jnp.sum((blk_end[None, :] <= blk[:, None]).astype(I32), axis=1), N_EXPERTS - 1)

    last_e = expert_of((total_blk - 1)[None])[0]
    e = expert_of(blk0)
    c_e = jnp.maximum(blk_cnt[e], 1)
    local = s - blk_start[e] * nj
    j = local // c_e
    r = local % c_e
    steps = dict(
        x_blk=jnp.where(valid, blk_start[e] + r, 0),
        o_blk=jnp.where(valid, blk_start[e] + r, blk0),
        o_j=jnp.where(valid, j, s % nj),
        w_e=jnp.where(valid, e, last_e),
        w_j=jnp.where(valid, j, nj - 1),
        first=(valid & (r == 0)).astype(I32),
        rows=jnp.where(valid, jnp.clip(counts[e] - r * rb, 0, rb), 0).astype(I32),
    )
    return seg, steps


def _aligned(row):
    return row if isinstance(row, int) else pl.multiple_of(row, SEG_ALIGN)


def _segment_copies(cnt, fn, max_rows=SEG_MAX):
    for b in range(SEG_ALIGN.bit_length() - 1, max_rows.bit_length()):
        size = 1 << b
        start = (cnt >> (b + 1)) << (b + 1)

        @pl.when(((cnt >> b) & 1) == 1)
        def _(start=start, size=size):
            fn(start, size)


def _start_split(cp):
    cp.start(priority=cp.dst_ref.shape[0].bit_length() % 2)


def _dispatch_kernel(start_ref, cnt_ref, pad_ref, rt_ref, h_ref, xs_ref, buf_ref, zero_ref, sem):
    b = pl.program_id(0)
    pos_t = jnp.transpose(rt_ref[...].astype(F32)).astype(I32)
    slot = lax.broadcasted_iota(I32, (SLOT_ROWS, ROUTE_ROWS), 0)
    sel = jnp.zeros((SLOT_ROWS, ROUTE_ROWS), F32)
    for k in range(TOP_K):
        sel += (slot == pos_t[TOP_K + k:TOP_K + k + 1, :]).astype(F32)
    xs = _bdot(sel.astype(BF16), h_ref[...])

    def copy(src_ref, src, dst, size, sem_ref):
        return pltpu.make_async_copy(src_ref.at[pl.ds(_aligned(src), size)],
                                     xs_ref.at[pl.ds(_aligned(dst), size)], sem_ref)

    def for_segments(blk, fn):
        par = blk % 2

        def body(e, off):
            c = cnt_ref[blk * N_EXPERTS + e]
            dst = start_ref[blk * N_EXPERTS + e]
            _segment_copies(c, lambda st, size: fn(copy(buf_ref.at[par], off + st, dst + st, size, sem.at[par])))
            return off + c
        lax.fori_loop(0, N_EXPERTS, body, 0)

    def wait_block(blk):
        par = blk % 2
        _segment_copies(cnt_ref[N_RBLK * N_EXPERTS + blk],
                        lambda st, size: copy(buf_ref.at[par], 0, 0, size, sem.at[par]).wait(), SLOT_ROWS)

    @pl.when(b >= 2)
    def _():
        wait_block(b - 2)

    buf_ref[b % 2] = _pack_bf16(xs, is_bf16=True)
    for_segments(b, _start_split)

    @pl.when(b == pl.num_programs(0) - 1)
    def _():
        @pl.when(b >= 1)
        def _():
            wait_block(b - 1)
        wait_block(b)
        zero_ref[...] = jnp.zeros(zero_ref.shape, zero_ref.dtype)
        zsem = sem.at[0]

        def for_pads(fn):
            def body(e, carry):
                dst = pad_ref[e]
                _segment_copies(pad_ref[N_EXPERTS + e],
                                lambda st, size: fn(copy(zero_ref, 0, dst + st, size, zsem)))
                return carry
            lax.fori_loop(0, N_EXPERTS, body, 0)

        for_pads(lambda cp: cp.start())
        for_pads(lambda cp: cp.wait())

        tail0 = pad_ref[2 * N_EXPERTS]
        ntail = pad_ref[2 * N_EXPERTS + 1]

        def tail_copy(i):
            return copy(zero_ref, 0, tail0 + i * SEG_MAX, SEG_MAX, zsem)

        lax.fori_loop(0, ntail, lambda i, c: (tail_copy(i).start(), c)[1], 0)
        lax.fori_loop(0, ntail, lambda i, c: (tail_copy(i).wait(), c)[1], 0)


def dispatch(h2, rt, seg):
    nb = h2.shape[0] // ROUTE_ROWS
    return pl.pallas_call(
        _dispatch_kernel,
        grid_spec=pltpu.PrefetchScalarGridSpec(
            num_scalar_prefetch=3,
            grid=(nb,),
            in_specs=[pl.BlockSpec((ROUTE_ROWS, 128), lambda b, *_: (b, 0)),
                      pl.BlockSpec((ROUTE_ROWS, D_MODEL), lambda b, *_: (b, 0))],
            out_specs=pl.BlockSpec(memory_space=pl.ANY),
            scratch_shapes=[pltpu.VMEM((2, SLOT_ROWS, D_MODEL // 2), U32),
                            pltpu.VMEM((SEG_MAX, D_MODEL // 2), U32),
                            pltpu.SemaphoreType.DMA((2,))],
        ),
        out_shape=jax.ShapeDtypeStruct((N_SLOTS, D_MODEL // 2), U32),
        compiler_params=_cparams("arbitrary"),
        name="dispatch",
    )(seg["start"], seg["cnt"], seg["pad"], rt, h2)


def _pack_bf16(x, is_bf16=False):
    if not is_bf16:
        x = x.astype(BF16).astype(F32)
    bits = lax.bitcast_convert_type(x, U32)
    w = x.shape[1] // 2
    return bits[:, w:] | (bits[:, :w] >> 16)


def _unpack_bf16(words):
    lo = lax.bitcast_convert_type(words << 16, F32).astype(BF16)
    hi = lax.bitcast_convert_type(words & jnp.uint32(0xFFFF0000), F32).astype(BF16)
    return lo, hi


def _for_row_counts(rows, o_ref, fn):
    for nc in range(1, EXPERT_ROWS // EXPERT_CHUNK + 1):
        n = nc * EXPERT_CHUNK

        @pl.when((rows + EXPERT_CHUNK - 1) // EXPERT_CHUNK == nc)
        def _(n=n):
            o_ref[:n, :] = fn(n).astype(o_ref.dtype)
            if n < EXPERT_ROWS:
                o_ref[n:, :] = jnp.zeros((EXPERT_ROWS - n, o_ref.shape[1]), o_ref.dtype)

    @pl.when(rows == 0)
    def _():
        o_ref[...] = jnp.zeros(o_ref.shape, o_ref.dtype)


def _expert_up_kernel(xb_ref, ob_ref, oj_ref, we_ref, wj_ref, first_ref, rows_ref,
                      x_ref, wg_ref, wu_ref, bg_ref, bu_ref, o_ref, wgb_ref, wub_ref):
    s = pl.program_id(0)

    @pl.when(first_ref[s] == 1)
    def _():
        wgb_ref[...] = wg_ref[...].astype(BF16)
        wub_ref[...] = wu_ref[...].astype(BF16)

    def act(n):
        lo, hi = _unpack_bf16(x_ref[:n, :])
        half = D_MODEL // 2
        g = _bdot(lo, wgb_ref[:half, :]) + _bdot(hi, wgb_ref[half:, :]) + bg_ref[...]
        u = _bdot(lo, wub_ref[:half, :]) + _bdot(hi, wub_ref[half:, :]) + bu_ref[...]
        g = jnp.minimum(g, SWIGLU_LIMIT)
        u = jnp.clip(u, -SWIGLU_LIMIT, SWIGLU_LIMIT)
        return g * jax.nn.sigmoid(SWIGLU_ALPHA * g) * (u + 1.0)

    _for_row_counts(rows_ref[s], o_ref, act)


def expert_up(layer, xs, w_gu, b_gu, steps):
    tn = EXPERT_TN
    nj = D_EXPERT // tn
    n_steps = N_BLK * nj
    wspec = lambda up: pl.BlockSpec(
        (None, None, D_MODEL, tn), lambda s, xb, ob, oj, we, wj, fi, ro: (layer, we[s], 0, up * nj + wj[s]))
    bspec = lambda up: pl.BlockSpec(
        (None, None, 1, tn), lambda s, xb, ob, oj, we, wj, fi, ro: (layer, we[s], 0, up * nj + wj[s]))
    return pl.pallas_call(
        _expert_up_kernel,
        grid_spec=pltpu.PrefetchScalarGridSpec(
            num_scalar_prefetch=7,
            grid=(n_steps,),
            in_specs=[
                pl.BlockSpec((EXPERT_ROWS, D_MODEL // 2), lambda s, xb, ob, oj, we, wj, fi, ro: (xb[s], 0)),
                wspec(0), wspec(1), bspec(0), bspec(1),
            ],
            out_specs=pl.BlockSpec((EXPERT_ROWS, tn), lambda s, xb, ob, oj, we, wj, fi, ro: (ob[s], oj[s])),
            scratch_shapes=[pltpu.VMEM((D_MODEL, tn), BF16), pltpu.VMEM((D_MODEL, tn), BF16)],
        ),
        out_shape=jax.ShapeDtypeStruct((N_SLOTS, D_EXPERT), BF16),
        compiler_params=_cparams("arbitrary"),
        name="expert_up",
    )(steps["x_blk"], steps["o_blk"], steps["o_j"], steps["w_e"], steps["w_j"], steps["first"],
      steps["rows"], xs, w_gu, w_gu, b_gu, b_gu)


def _expert_down_kernel(xb_ref, ob_ref, oj_ref, we_ref, wj_ref, first_ref, rows_ref,
                        a_ref, w_ref, b_ref, o_ref, wb_ref):
    s = pl.program_id(0)

    @pl.when(first_ref[s] == 1)
    def _():
        wb_ref[...] = w_ref[...].astype(BF16)

    def down(n):
        y = _bdot(a_ref[:n, :], wb_ref[...]) + b_ref[...]
        return _pack_bf16(y)

    _for_row_counts(rows_ref[s], o_ref, down)


def expert_down(layer, act, w_dn, b_dn, steps):
    tn = EXPERT_TN
    nj = D_MODEL // tn
    n_steps = N_BLK * nj
    return pl.pallas_call(
        _expert_down_kernel,
        grid_spec=pltpu.PrefetchScalarGridSpec(
            num_scalar_prefetch=7,
            grid=(n_steps,),
            in_specs=[
                pl.BlockSpec((EXPERT_ROWS, D_EXPERT), lambda s, xb, ob, oj, we, wj, fi, ro: (xb[s], 0)),
                pl.BlockSpec((None, None, D_EXPERT, tn),
                             lambda s, xb, ob, oj, we, wj, fi, ro: (layer, we[s], 0, wj[s])),
                pl.BlockSpec((None, None, 1, tn),
                             lambda s, xb, ob, oj, we, wj, fi, ro: (layer, we[s], 0, wj[s])),
            ],
            out_specs=pl.BlockSpec((EXPERT_ROWS, tn // 2),
                                   lambda s, xb, ob, oj, we, wj, fi, ro: (ob[s], oj[s])),
            scratch_shapes=[pltpu.VMEM((D_EXPERT, tn), BF16)],
        ),
        out_shape=jax.ShapeDtypeStruct((N_SLOTS, D_MODEL // 2), U32),
        compiler_params=_cparams("arbitrary"),
        name="expert_down",
    )(steps["x_blk"], steps["o_blk"], steps["o_j"], steps["w_e"], steps["w_j"], steps["first"],
      steps["rows"], act, w_dn, b_dn)


def _combine_kernel(final, start_ref, cnt_ref, rt_ref, gt_ref, x_ref, g2_ref, nf_ref, ys_ref,
                    o_ref, buf_ref, sem):
    b = pl.program_id(0)

    def for_segments(blk, fn):
        par = blk % 2

        def body(e, off):
            c = cnt_ref[blk * N_EXPERTS + e]
            src = start_ref[blk * N_EXPERTS + e]
            _segment_copies(c, lambda st, size: fn(pltpu.make_async_copy(
                ys_ref.at[pl.ds(_aligned(src + st), size)],
                buf_ref.at[par, pl.ds(_aligned(off + st), size)], sem.at[par])))
            return off + c
        lax.fori_loop(0, N_EXPERTS, body, 0)

    @pl.when(b == 0)
    def _():
        tail = ROUTE_ROWS * TOP_K
        buf_ref[:, tail:, :] = jnp.zeros((2, SLOT_ROWS - tail, buf_ref.shape[2]), U32)
        for_segments(b, _start_split)

    @pl.when(b + 1 < pl.num_programs(0))
    def _():
        for_segments(b + 1, _start_split)

    _segment_copies(cnt_ref[N_RBLK * N_EXPERTS + b],
                    lambda st, size: pltpu.make_async_copy(
                        ys_ref.at[pl.ds(0, size)], buf_ref.at[b % 2, pl.ds(0, size)], sem.at[b % 2]).wait(),
                    SLOT_ROWS)

    rt = rt_ref[...]
    gt = gt_ref[...]
    slot = lax.broadcasted_iota(I32, (ROUTE_ROWS, SLOT_ROWS), 1)
    w = jnp.zeros((ROUTE_ROWS, SLOT_ROWS), F32)
    for k in range(TOP_K):
        w += jnp.where(slot == rt[:, TOP_K + k:TOP_K + k + 1], gt[:, k:k + 1], 0.0)
    w_hi = w.astype(BF16)
    w_lo = (w - w_hi.astype(F32)).astype(BF16)
    y_lo, y_hi = _unpack_bf16(buf_ref[b % 2])
    a_lo = _bdot(w_hi, y_lo) + _bdot(w_lo, y_lo)
    a_hi = _bdot(w_hi, y_hi) + _bdot(w_lo, y_hi)
    hw = EXPERT_TN // 2
    acc = jnp.concatenate(
        [part[:, j * hw:(j + 1) * hw] for j in range(D_MODEL // EXPERT_TN) for part in (a_lo, a_hi)], axis=1)
    x2 = x_ref[...] + g2_ref[...] * acc
    if final:
        x2 = _rms(x2, nf_ref[...])
    o_ref[...] = x2


def combine(ys, rt, gates, seg, x1, g2, norm_f, final):
    tb = ROUTE_ROWS
    t = x1.shape[0]
    return pl.pallas_call(
        functools.partial(_combine_kernel, final),
        grid_spec=pltpu.PrefetchScalarGridSpec(
            num_scalar_prefetch=2,
            grid=(t // tb,),
            in_specs=[
                pl.BlockSpec((tb, 128), lambda i, *_: (i, 0)),
                pl.BlockSpec((tb, 128), lambda i, *_: (i, 0)),
                pl.BlockSpec((tb, D_MODEL), lambda i, *_: (i, 0)),
                pl.BlockSpec((None, 1, D_MODEL), lambda i, *_: (_mod_row(i, tb), 0, 0)),
                pl.BlockSpec((1, D_MODEL), lambda i, *_: (0, 0)),
                pl.BlockSpec(memory_space=pl.ANY),
            ],
            out_specs=pl.BlockSpec((tb, D_MODEL), lambda i, *_: (i, 0)),
            scratch_shapes=[pltpu.VMEM((2, SLOT_ROWS, D_MODEL // 2), U32), pltpu.SemaphoreType.DMA((2,))],
        ),
        out_shape=jax.ShapeDtypeStruct((t, D_MODEL), F32),
        compiler_params=_cparams("arbitrary"),
        name="combine",
    )(seg["start"], seg["cnt"], rt, gates, x1, g2, norm_f, ys)


def _rope_tables():
    rows = DEC_SEQ // GRID_W
    row = jnp.repeat(jnp.arange(rows, dtype=F32), GRID_W)
    col = jnp.tile(jnp.arange(GRID_W, dtype=F32), rows)
    half = QK_ROPE // 2
    inv = 1.0 / (ROPE_THETA ** (jnp.arange(0, half, 2, dtype=F32) / half))
    ar = row[:, None] * inv
    ac = col[:, None] * inv
    ang = jnp.concatenate([ar, ar, ac, ac], axis=-1)
    cos = jnp.concatenate([jnp.ones((SEQ, QK_ROPE), F32), jnp.cos(ang)], axis=0)
    sin = jnp.concatenate([jnp.zeros((SEQ, QK_ROPE), F32), jnp.sin(ang)], axis=0)
    scale = (QK_NOPE + QK_ROPE) ** -0.5
    tab_q = jnp.concatenate([jnp.ones((TAB_ROWS, QK_NOPE), F32), cos, sin], axis=-1) * scale
    tab_k = jnp.concatenate([cos, sin], axis=-1)
    return tab_q, tab_k


def _rotate_cols(w):
    r1, r2, c1, c2 = jnp.split(w, 4, axis=-1)
    return jnp.concatenate([-r2, r1, -c2, c1], axis=-1)


def _layer_weights(w_in, w_uq, w_uk, w_uv, w_out, s5_w_glu, w_router, b_router):
    hy, q, kv, kr, s5 = jnp.split(w_in, [HY_PROJ, HY_PROJ + Q_RANK, HY_PROJ + Q_RANK + KV_RANK,
                                         HY_PROJ + Q_RANK + KV_RANK + QK_ROPE], axis=-1)
    w_in_aug = jnp.concatenate([hy, q, kv, kr, _rotate_cols(kr), s5], axis=-1).astype(BF16)
    wq = w_uq.reshape(Q_RANK, MLA_HEADS, QK_NOPE + QK_ROPE)
    wq_rope = wq[..., QK_NOPE:]
    w_q = jnp.concatenate([wq[..., :QK_NOPE], wq_rope, _rotate_cols(wq_rope)], axis=-1)
    w_q = w_q.reshape(Q_RANK, MLA_HEADS * QH).astype(BF16)
    wk = w_uk.reshape(KV_RANK, MLA_HEADS, QK_NOPE)
    wk = jnp.concatenate([wk, jnp.zeros((KV_RANK, MLA_HEADS, 2 * QK_ROPE), F32)], axis=-1)
    eye = jnp.eye(QK_ROPE, dtype=F32)
    ek = jnp.concatenate([jnp.zeros((QK_ROPE, QK_NOPE), F32), eye, eye], axis=-1)
    ek = jnp.broadcast_to(ek[:, None, :], (QK_ROPE, MLA_HEADS, QH))
    wk = jnp.concatenate([wk, ek, jnp.zeros((QK_ROPE, MLA_HEADS, QH), F32)], axis=0)
    wv = jnp.concatenate([w_uv, jnp.zeros((2 * QK_ROPE, MLA_W), F32)], axis=0)
    w_kv = jnp.concatenate([wk.reshape(KVR_W, MLA_HEADS * QH), wv], axis=-1).astype(BF16)
    w_r = jnp.concatenate([w_router, jnp.zeros((D_MODEL, 128 - N_EXPERTS), F32)], axis=-1)
    b_r = jnp.concatenate([b_router, jnp.zeros((128 - N_EXPERTS,), F32)])[None, :]
    w_r_hi, w_r_lo = _split_bf16(w_r)
    return w_in_aug, w_q, w_kv, w_out.astype(BF16), s5_w_glu.astype(BF16), w_r_hi, w_r_lo, b_r


def _s5_block_diag(lbr, lbi, bbr, bbi, c_re, c_im):
    nslab = S5_NS // S5_LANES
    gps = S5_GROUPS // nslab
    eye = jnp.eye(gps, dtype=F32)
    lead = bbr.shape[:2]

    def bd_in(b):
        b = b.reshape(lead + (nslab, gps, S5_STATE, S5_GROUP_CH))
        return jnp.einsum("ldcgpn,gh->ldcgnhp", b, eye).reshape(lead + (nslab, S5_CH, S5_LANES)).astype(BF16)

    def bd_out(c):
        c = c.reshape(lead + (nslab, gps, S5_GROUP_CH, S5_STATE))
        return jnp.einsum("ldcgnp,gh->ldcgphn", c, eye).reshape(lead + (nslab, S5_LANES, S5_CH)).astype(BF16)

    lam = jnp.concatenate([lbr.reshape(lead + (1, S5_NS)), lbi.reshape(lead + (1, S5_NS))], axis=-1)
    lam8 = jnp.broadcast_to(lam, lead + (S5_SUB, 2 * S5_NS))
    return bd_in(bbr), bd_in(bbi), bd_out(c_re), bd_out(c_im), lam8


def kernel(x_prompt, x_sample, cache_ckv, cache_krope, state_ssm, c, c_ctx, w_ada, b_ada, norm1, norm2, w_in, w_out, hy_conv_w, hy_conv_b, hy_fw1, hy_fb1, hy_fw2, hy_fb2, hy_freq, hy_fw3, hy_decay, hy_skip, q_norm, kv_norm, w_uq, w_uk, w_uv, s5_lam_re, s5_lam_im, s5_log_dt, s5_b_re, s5_b_im, s5_c_re, s5_c_im, s5_d, s5_w_glu, s5_b_glu, w_router, b_router, w_gate_up, b_gate_up, w_down, b_down, norm_f):
    x = jnp.concatenate([x_prompt.reshape(T_P, D_MODEL), x_sample.reshape(T_S, D_MODEL)], axis=0)

    cond8 = jnp.concatenate([c_ctx[None, :], c, jnp.zeros((N_MOD - 1 - DEC_BATCH, D_MODEL), F32)], axis=0)
    mod = ada_modulation(cond8, w_ada, b_ada).reshape(DEPTH, N_MOD, 6, 1, D_MODEL)

    tab_q, tab_k = _rope_tables()
    dft = {}
    for n in (SEQ, DEC_SEQ):
        fwd, inv = _dft_tables(n)
        f_hi, f_lo = _split_bf16(fwd)
        dft[n] = (f_hi, f_lo, jnp.asarray(inv, F32).astype(BF16))
    fw1p = jnp.concatenate([hy_fw1, jnp.zeros((DEPTH, FILT_HID - POS_EMB, FILT_HID), F32)], axis=1)
    spectra = {n: hyena_spectra(n, fw1p, hy_fb1[:, None, :], hy_fw2, hy_fb2[:, None, :], hy_freq,
                                hy_fw3, hy_decay[:, None, :], dft[n][0], dft[n][1])
               for n in (SEQ, DEC_SEQ)}

    lbr, lbi, bbr, bbi = s5_params(s5_lam_re, s5_lam_im, s5_log_dt, s5_b_re, s5_b_im)
    s5_maps = _s5_block_diag(lbr, lbi, bbr, bbi, s5_c_re, s5_c_im)
    b_gu4 = b_gate_up.reshape(DEPTH, N_EXPERTS, 1, 2 * D_EXPERT)
    b_dn4 = b_down.reshape(DEPTH, N_EXPERTS, 1, D_MODEL)
    bp_s = S5_SUB

    ckv_out, kr_out, ssm_out = [], [], []
    for l in range(DEPTH):
        w_in_aug, w_q, w_kv, w_out_b, w_glu_b, w_r_hi, w_r_lo, b_r = _layer_weights(
            w_in[l], w_uq[l], w_uk[l], w_uv[l], w_out[l], s5_w_glu[l], w_router[l], b_router[l])
        m = lambda k: mod[l, :, k]
        z_hy, z_q, z_kvr, z_s5 = in_proj(x, norm1[l][None, :], m(1), m(0), w_in_aug)

        cw, cb, sk = hy_conv_w[l], hy_conv_b[l][None, :], hy_skip[l]
        y_hy = hyena(z_hy, cw, cb, dft[SEQ][0], dft[SEQ][2], spectra[SEQ][l], sk,
                     n=SEQ, nbatch=BATCH, row0=0, nseq=8, t_out=T_ALL,
                     prev=jnp.zeros((T_ALL, HY_W), BF16))
        y_hy = hyena(z_hy, cw, cb, dft[DEC_SEQ][0], dft[DEC_SEQ][2], spectra[DEC_SEQ][l], sk,
                     n=DEC_SEQ, nbatch=DEC_BATCH, row0=T_P, nseq=1, t_out=T_ALL, prev=y_hy)

        q = q_proj(z_q, q_norm[l][None, :], w_q, tab_q)
        a = kv_prep(z_kvr, kv_norm[l][None, :], tab_k)
        a_p = a[:T_P]
        ckv_out.append(a_p[:, :KV_RANK].reshape(BATCH, SEQ, KV_RANK))
        kr_out.append(a_p[:, KV_RANK:KV_RANK + QK_ROPE].reshape(BATCH, SEQ, QK_ROPE))
        ctx = jnp.concatenate([cache_ckv[:, l], cache_krope[:, l],
                               jnp.zeros((DEC_BATCH, PAST_LEN, QK_ROPE), F32)], axis=-1)
        a_s = jnp.concatenate([a[T_P:].reshape(DEC_BATCH, DEC_SEQ, KVR_W), ctx], axis=1)
        lk_s = DEC_SEQ + PAST_LEN
        k_p, v_p = kv_up(a_p, w_kv)
        k_s, v_s = kv_up(a_s.reshape(DEC_BATCH * lk_s, KVR_W), w_kv)
        y_att = attention(q, k_p, v_p, nbatch=BATCH, lq=SEQ, lk=SEQ, row0=0, t_out=T_ALL,
                          prev=jnp.zeros((T_ALL, MLA_W), BF16))
        y_att = attention(q, k_s, v_s, nbatch=DEC_BATCH, lq=DEC_SEQ, lk=lk_s, row0=T_P, t_out=T_ALL,
                          prev=y_att)

        st = state_ssm[:, l].reshape(DEC_BATCH, 2, 2 * S5_NS).transpose(1, 0, 2)
        x0_s = jnp.concatenate([st, jnp.zeros((2, bp_s - DEC_BATCH, 2 * S5_NS), F32)], axis=1)
        x0_p = jnp.zeros((2, BATCH, 2 * S5_NS), F32)
        y2_p, xf_p = s5_scan(z_s5, *s5_maps, x0_p, layer=l, nbatch=BATCH, n=SEQ, row0=0)
        y2_s, _ = s5_scan(z_s5, *s5_maps, x0_s, layer=l, nbatch=DEC_BATCH, n=DEC_SEQ, row0=T_P)
        ssm_out.append(xf_p.reshape(2, BATCH, 2, S5_GROUPS, S5_STATE).transpose(1, 0, 2, 3, 4))
        d_row, bg_row = s5_d[l][None, :], s5_b_glu[l][None, :]
        y_s5 = s5_glu(y2_p.reshape(2, T_P, S5_W), z_s5, d_row, w_glu_b, bg_row,
                      jnp.zeros((T_ALL, S5_W), BF16), row0=0)
        y_s5 = s5_glu(y2_s.reshape(2, T_S, S5_W), z_s5, d_row, w_glu_b, bg_row, y_s5, row0=T_P)

        x1, h2, rt, gates, cnt = out_proj(y_hy, y_att, y_s5, w_out_b, x, m(2), norm2[l][None, :],
                                          m(4), m(3), w_r_hi, w_r_lo, b_r)

        seg, steps = route_tables(cnt)
        xs = dispatch(h2, rt, seg)
        act = expert_up(l, xs, w_gate_up, b_gu4, steps)
        ys = expert_down(l, act, w_down, b_dn4, steps)
        x = combine(ys, rt, gates, seg, x1, m(5), norm_f[None, :], final=(l == DEPTH - 1))

    y_prompt = x[:T_P].reshape(BATCH, SEQ, D_MODEL)
    y_sample = x[T_P:].reshape(DEC_BATCH, DEC_SEQ, D_MODEL)
    return (y_prompt, y_sample, jnp.stack(ckv_out, axis=1), jnp.stack(kr_out, axis=1),
            jnp.stack(ssm_out, axis=1))
```
